```python
import jax, jax.numpy as jnp
from jax import lax
import numpy as np

D_MODEL = 4096
BATCH = 2
SEQ = 8192
DEPTH = 1

LRU_WIDTH = D_MODEL // 2
LRU_HEADS = 16
LRU_HEAD_DIM = LRU_WIDTH // LRU_HEADS
LRU_CONV_WIDTH = 4
LRU_C = 8.0
RWKV_WIDTH = D_MODEL - LRU_WIDTH
RWKV_HEAD_DIM = 64
RWKV_HEADS = RWKV_WIDTH // RWKV_HEAD_DIM
DECAY_RANK = 96
ICLR_RANK = 96
GATE_RANK = 256
MIX_WIDTH = LRU_WIDTH + RWKV_WIDTH
RWKV_PROJ_WIDTH = 3 * RWKV_WIDTH + DECAY_RANK + ICLR_RANK + GATE_RANK
IN_PROJ_WIDTH = 2 * LRU_WIDTH + RWKV_PROJ_WIDTH
D_FF = 11008
FFN_CONV_WIDTH = 3
NORM_EPS = 1e-6
GN_EPS = 64e-5
L2_EPS = 1e-12

kernel_name = "hymba_rglru_rwkv7_convffn"


def rmsnorm(x, g):
    xf = x.astype(jnp.float32)
    out = xf * lax.rsqrt(jnp.mean(xf * xf, axis=-1, keepdims=True) + NORM_EPS)
    return (out * g.astype(jnp.float32)).astype(x.dtype)


def causal_dwconv(x, w, b):
    width = w.shape[0]
    seq = x.shape[1]
    xp = jnp.pad(x, ((0, 0), (width - 1, 0), (0, 0)))
    return sum(xp[:, j:j + seq] * w[j] for j in range(width)) + b


def token_shift(z):
    return jnp.pad(z[:, :-1], ((0, 0), (1, 0), (0, 0)))


def rg_lru(x, w_a, b_a, w_i, b_i, lam):
    bsz, seq, _ = x.shape
    xf = x.astype(jnp.float32)
    xh = xf.reshape(bsz, seq, LRU_HEADS, LRU_HEAD_DIM)
    r = jax.nn.sigmoid(jnp.einsum('bshi,hij->bshj', xh, w_a.astype(jnp.float32)) + b_a).reshape(bsz, seq, LRU_WIDTH)
    i = jax.nn.sigmoid(jnp.einsum('bshi,hij->bshj', xh, w_i.astype(jnp.float32)) + b_i).reshape(bsz, seq, LRU_WIDTH)
    log_a = -LRU_C * r * jax.nn.softplus(-lam.astype(jnp.float32))
    a = jnp.exp(log_a)
    u = jnp.sqrt(-jnp.expm1(2.0 * log_a)) * (i * xf)

    def combine(left, right):
        a_l, b_l = left
        a_r, b_r = right
        return a_l * a_r, a_r * b_l + b_r

    _, h = lax.associative_scan(combine, (a, u), axis=1)
    return h.astype(x.dtype)


def rwkv7_step(state, inp):
    r, w, k, v, kk, a = inp
    sa = jnp.einsum('bhij,bhj->bhi', state, kk)
    state = (state * w[:, :, None, :]
             - sa[..., None] * (kk * a)[:, :, None, :]
             + v[..., None] * k[:, :, None, :])
    y = jnp.einsum('bhij,bhj->bhi', state, r)
    return state, y


def rwkv7_mix(z, mu, w0, w2, a0, a2, g2, k_k, k_a, r_k, gn_w, gn_b):
    bsz, seq, _ = z.shape
    z = z + (token_shift(z) - z) * mu
    idx = np.cumsum([RWKV_WIDTH, RWKV_WIDTH, RWKV_WIDTH, DECAY_RANK, ICLR_RANK]).tolist()
    r, k, v, zw, za, zg = jnp.split(z, idx, axis=-1)
    f32 = jnp.float32
    w_log = -jax.nn.softplus(-(w0.astype(f32) + jnp.tanh(zw.astype(f32)) @ w2.astype(f32))) - 0.5
    decay = jnp.exp(-jnp.exp(w_log))
    a = jax.nn.sigmoid(a0.astype(f32) + za.astype(f32) @ a2.astype(f32))
    g = jax.nn.sigmoid(zg) @ g2

    def heads(t):
        return t.astype(f32).reshape(bsz, seq, RWKV_HEADS, RWKV_HEAD_DIM)

    r, k, v, decay, a = heads(r), heads(k), heads(v), heads(decay), heads(a)
    kk = k * k_k.astype(f32).reshape(RWKV_HEADS, RWKV_HEAD_DIM)
    kk = kk / jnp.maximum(jnp.sqrt(jnp.sum(kk * kk, axis=-1, keepdims=True)), L2_EPS)
    k = k * (1.0 + (a - 1.0) * k_a.astype(f32).reshape(RWKV_HEADS, RWKV_HEAD_DIM))

    xs = tuple(jnp.moveaxis(t, 1, 0) for t in (r, decay, k, v, kk, a))
    state0 = jnp.zeros((bsz, RWKV_HEADS, RWKV_HEAD_DIM, RWKV_HEAD_DIM), f32)
    _, y = lax.scan(rwkv7_step, state0, xs)
    y = jnp.moveaxis(y, 0, 1)

    mean = jnp.mean(y, axis=-1, keepdims=True)
    var = jnp.mean(jnp.square(y - mean), axis=-1, keepdims=True)
    y = (y - mean) * lax.rsqrt(var + GN_EPS)
    y = y * gn_w.astype(f32).reshape(RWKV_HEADS, RWKV_HEAD_DIM) + gn_b.astype(f32).reshape(RWKV_HEADS, RWKV_HEAD_DIM)
    bonus = jnp.sum(r * k * r_k.astype(f32).reshape(RWKV_HEADS, RWKV_HEAD_DIM), axis=-1, keepdims=True) * v
    out = (y + bonus).reshape(bsz, seq, RWKV_WIDTH).astype(z.dtype)
    return out * g


def setup_inputs(seed: int = 0) -> dict:
    key = jax.random.key(seed)
    ks = jax.random.split(key, 32)
    f32 = jnp.float32

    def nrm(k, shape, scale):
        return jax.random.normal(k, shape, f32) * scale

    a_c = jax.random.uniform(ks[10], (DEPTH, LRU_WIDTH), f32, 0.9, 0.999)
    a_init = a_c ** (1.0 / LRU_C)
    lam = jnp.log(a_init) - jnp.log1p(-a_init)
    return {
        "x": nrm(ks[0], (BATCH, SEQ, D_MODEL), 1.0),
        "g_mix": 1.0 + nrm(ks[1], (DEPTH, D_MODEL), 0.02),
        "w_in": nrm(ks[2], (DEPTH, D_MODEL, IN_PROJ_WIDTH), D_MODEL ** -0.5),
        "conv_lru_w": nrm(ks[3], (DEPTH, LRU_CONV_WIDTH, LRU_WIDTH), LRU_CONV_WIDTH ** -0.5),
        "conv_lru_b": nrm(ks[4], (DEPTH, LRU_WIDTH), 0.01),
        "lru_w_a": nrm(ks[5], (DEPTH, LRU_HEADS, LRU_HEAD_DIM, LRU_HEAD_DIM), LRU_HEAD_DIM ** -0.5),
        "lru_b_a": nrm(ks[6], (DEPTH, LRU_HEADS, LRU_HEAD_DIM), 0.01),
        "lru_w_i": nrm(ks[7], (DEPTH, LRU_HEADS, LRU_HEAD_DIM, LRU_HEAD_DIM), LRU_HEAD_DIM ** -0.5),
        "lru_b_i": nrm(ks[8], (DEPTH, LRU_HEADS, LRU_HEAD_DIM), 0.01),
        "lru_lambda": lam,
        "rwkv_mu": jax.random.uniform(ks[11], (DEPTH, RWKV_PROJ_WIDTH), f32, 0.0, 1.0),
        "rwkv_w0": jax.random.uniform(ks[12], (DEPTH, RWKV_WIDTH), f32, -6.0, -1.0),
        "rwkv_w2": nrm(ks[13], (DEPTH, DECAY_RANK, RWKV_WIDTH), 0.1 * DECAY_RANK ** -0.5),
        "rwkv_a0": nrm(ks[14], (DEPTH, RWKV_WIDTH), 0.1),
        "rwkv_a2": nrm(ks[15], (DEPTH, ICLR_RANK, RWKV_WIDTH), 0.5 * ICLR_RANK ** -0.5),
        "rwkv_g2": nrm(ks[16], (DEPTH, GATE_RANK, RWKV_WIDTH), GATE_RANK ** -0.5),
        "rwkv_k_k": 0.85 + nrm(ks[17], (DEPTH, RWKV_WIDTH), 0.02),
        "rwkv_k_a": 1.0 + nrm(ks[18], (DEPTH, RWKV_WIDTH), 0.02),
        "rwkv_r_k": nrm(ks[19], (DEPTH, RWKV_WIDTH), 0.1),
        "rwkv_gn_w": 1.0 + nrm(ks[20], (DEPTH, RWKV_WIDTH), 0.02),
        "rwkv_gn_b": nrm(ks[21], (DEPTH, RWKV_WIDTH), 0.01),
        "w_out": nrm(ks[22], (DEPTH, MIX_WIDTH, D_MODEL), MIX_WIDTH ** -0.5),
        "g_ffn": 1.0 + nrm(ks[23], (DEPTH, D_MODEL), 0.02),
        "w_ffn_gate": nrm(ks[24], (DEPTH, D_MODEL, D_FF), D_MODEL ** -0.5),
        "ffn_conv_w": nrm(ks[25], (DEPTH, FFN_CONV_WIDTH, D_FF), FFN_CONV_WIDTH ** -0.5),
        "ffn_conv_b": nrm(ks[26], (DEPTH, D_FF), 0.01),
        "w_ffn_up": nrm(ks[27], (DEPTH, D_MODEL, D_FF), D_MODEL ** -0.5),
        "w_ffn_down": nrm(ks[28], (DEPTH, D_FF, D_MODEL), D_FF ** -0.5),
        "g_final": 1.0 + nrm(ks[29], (D_MODEL,), 0.02),
    }


def reference(x, g_mix, w_in, conv_lru_w, conv_lru_b, lru_w_a, lru_b_a, lru_w_i, lru_b_i,
              lru_lambda, rwkv_mu, rwkv_w0, rwkv_w2, rwkv_a0, rwkv_a2, rwkv_g2, rwkv_k_k,
              rwkv_k_a, rwkv_r_k, rwkv_gn_w, rwkv_gn_b, w_out, g_ffn, w_ffn_gate, ffn_conv_w,
              ffn_conv_b, w_ffn_up, w_ffn_down, g_final):
    for l in range(DEPTH):
        h = rmsnorm(x, g_mix[l])
        z = h @ w_in[l]
        z_lru_x = z[..., :LRU_WIDTH]
        z_lru_gate = z[..., LRU_WIDTH:2 * LRU_WIDTH]
        z_rwkv = z[..., 2 * LRU_WIDTH:]
        xc = causal_dwconv(z_lru_x, conv_lru_w[l], conv_lru_b[l])
        y_lru = rg_lru(xc, lru_w_a[l], lru_b_a[l], lru_w_i[l], lru_b_i[l], lru_lambda[l]) * jax.nn.gelu(z_lru_gate)
        y_rwkv = rwkv7_mix(z_rwkv, rwkv_mu[l], rwkv_w0[l], rwkv_w2[l], rwkv_a0[l], rwkv_a2[l],
                           rwkv_g2[l], rwkv_k_k[l], rwkv_k_a[l], rwkv_r_k[l], rwkv_gn_w[l], rwkv_gn_b[l])
        y = jnp.concatenate([y_lru, y_rwkv], axis=-1)
        x = x + y @ w_out[l]
        h = rmsnorm(x, g_ffn[l])
        gate = causal_dwconv(h @ w_ffn_gate[l], ffn_conv_w[l], ffn_conv_b[l])
        x = x + (jax.nn.silu(gate) * (h @ w_ffn_up[l])) @ w_ffn_down[l]
    return rmsnorm(x, g_final)
```

```python
import functools
import math

import jax
import jax.numpy as jnp
from jax import lax
from jax.experimental import pallas as pl
from jax.experimental.pallas import tpu as pltpu

F32 = jnp.float32
BF16 = jnp.bfloat16

V7X_VMEM_LIMIT_BYTES = 56 * 1024 * 1024
LANES = 128
SUBLANES = 8

LRU_HEAD_DIM = 128
LRU_CONV_WIDTH = 4
LRU_C = 8.0
RWKV_HEAD_DIM = 64
DECAY_RANK = 96
ICLR_RANK = 96
GATE_RANK = 256
FFN_CONV_WIDTH = 3
NORM_EPS = 1e-6
GN_EPS = 64e-5
L2_EPS = 1e-12
RANK_PAD = LANES
TAIL_WIDTH = 2 * RANK_PAD + GATE_RANK
RWKV_CHUNK = 64


def _cparams(*sem):
    return pltpu.CompilerParams(dimension_semantics=sem, vmem_limit_bytes=V7X_VMEM_LIMIT_BYTES)


def _dot(a, b):
    return jnp.dot(a, b, preferred_element_type=F32)


def _dot_nt(a, b):
    return lax.dot_general(a, b, (((1,), (1,)), ((), ())), preferred_element_type=F32)


def _dot_tn(a, b):
    return lax.dot_general(a, b, (((0,), (0,)), ((), ())), preferred_element_type=F32)


def _shift_rows(x, prev8, s):
    xe = jnp.concatenate([prev8, x], axis=0)
    return pltpu.roll(xe, s, 0)[SUBLANES:, :]


def _rmsnorm_kernel(x_ref, g_ref, o_ref):
    x = x_ref[...]
    ms = jnp.mean(x * x, axis=-1, keepdims=True)
    o_ref[...] = (x * lax.rsqrt(ms + NORM_EPS) * g_ref[...]).astype(o_ref.dtype)


def _rmsnorm(x2d, g, out_dtype, bm):
    m, d = x2d.shape
    return pl.pallas_call(
        _rmsnorm_kernel,
        out_shape=jax.ShapeDtypeStruct((m, d), out_dtype),
        grid=(m // bm,),
        in_specs=[pl.BlockSpec((bm, d), lambda i: (i, 0)),
                  pl.BlockSpec((1, d), lambda i: (0, 0))],
        out_specs=pl.BlockSpec((bm, d), lambda i: (i, 0)),
        compiler_params=_cparams("parallel"),
        name="rmsnorm",
    )(x2d, g.reshape(1, d))


def _mm_kernel(a_ref, b_ref, o_ref):
    o_ref[...] = _dot(a_ref[...], b_ref[...]).astype(o_ref.dtype)


def _matmul(a, b, out_dtype, bm, bn, name):
    m, k = a.shape
    n = b.shape[1]
    return pl.pallas_call(
        _mm_kernel,
        out_shape=jax.ShapeDtypeStruct((m, n), out_dtype),
        grid=(m // bm, n // bn),
        in_specs=[pl.BlockSpec((bm, k), lambda i, j: (i, 0)),
                  pl.BlockSpec((k, bn), lambda i, j: (0, j))],
        out_specs=pl.BlockSpec((bm, bn), lambda i, j: (i, j)),
        compiler_params=_cparams("parallel", "parallel"),
        name=name,
    )(a, b)


def _mm_res_kernel(a_ref, b_ref, r_ref, o_ref):
    o_ref[...] = r_ref[...] + _dot(a_ref[...], b_ref[...])


def _matmul_residual(a, b, res, bm, bn, name):
    m, k = a.shape
    n = b.shape[1]
    return pl.pallas_call(
        _mm_res_kernel,
        out_shape=jax.ShapeDtypeStruct((m, n), F32),
        grid=(m // bm, n // bn),
        in_specs=[pl.BlockSpec((bm, k), lambda i, j: (i, 0)),
                  pl.BlockSpec((k, bn), lambda i, j: (0, j)),
                  pl.BlockSpec((bm, bn), lambda i, j: (i, j))],
        out_specs=pl.BlockSpec((bm, bn), lambda i, j: (i, j)),
        compiler_params=_cparams("parallel", "parallel"),
        name=name,
    )(a, b, res)


def _outproj_kernel(a1_ref, a2_ref, w1_ref, w2_ref, r_ref, o_ref):
    acc = _dot(a1_ref[...], w1_ref[...]) + _dot(a2_ref[...], w2_ref[...])
    o_ref[...] = r_ref[...] + acc


def _outproj(y1, y2, w, res, bm, bn):
    m, k1 = y1.shape
    k2 = y2.shape[1]
    assert k1 == k2
    n = w.shape[1]
    return pl.pallas_call(
        _outproj_kernel,
        out_shape=jax.ShapeDtypeStruct((m, n), F32),
        grid=(m // bm, n // bn),
        in_specs=[pl.BlockSpec((bm, k1), lambda i, j: (i, 0)),
                  pl.BlockSpec((bm, k2), lambda i, j: (i, 0)),
                  pl.BlockSpec((k1, bn), lambda i, j: (0, j)),
                  pl.BlockSpec((k2, bn), lambda i, j: (1, j)),
                  pl.BlockSpec((bm, bn), lambda i, j: (i, j))],
        out_specs=pl.BlockSpec((bm, bn), lambda i, j: (i, j)),
        compiler_params=_cparams("parallel", "parallel"),
        name="outproj",
    )(y1, y2, w, w, res)


def _ffn_up_kernel(h_ref, wg_ref, wu_ref, cw_ref, cb_ref, o_ref, prev_ref, *, tiles_per_seq):
    i = pl.program_id(1)

    @pl.when(i % tiles_per_seq == 0)
    def _():
        prev_ref[...] = jnp.zeros_like(prev_ref)

    h = h_ref[...]
    gate = _dot(h, wg_ref[...])
    up = _dot(h, wu_ref[...])
    prev8 = prev_ref[...]
    prev_ref[...] = gate[-SUBLANES:, :]
    cw = cw_ref[...]
    conv = (cw[2:3, :] * gate + cw[1:2, :] * _shift_rows(gate, prev8, 1)
            + cw[0:1, :] * _shift_rows(gate, prev8, 2) + cb_ref[...])
    o_ref[...] = (conv * jax.nn.sigmoid(conv) * up).astype(o_ref.dtype)


def _ffn_up(h, wg, wu, conv_w, conv_b, seq, bm, bn):
    m, k = h.shape
    n = wg.shape[1]
    kern = functools.partial(_ffn_up_kernel, tiles_per_seq=seq // bm)
    return pl.pallas_call(
        kern,
        out_shape=jax.ShapeDtypeStruct((m, n), BF16),
        grid=(n // bn, m // bm),
        in_specs=[pl.BlockSpec((bm, k), lambda j, i: (i, 0)),
                  pl.BlockSpec((k, bn), lambda j, i: (0, j)),
                  pl.BlockSpec((k, bn), lambda j, i: (0, j)),
                  pl.BlockSpec((FFN_CONV_WIDTH, bn), lambda j, i: (0, j)),
                  pl.BlockSpec((1, bn), lambda j, i: (0, j))],
        out_specs=pl.BlockSpec((bm, bn), lambda j, i: (i, j)),
        scratch_shapes=[pltpu.VMEM((SUBLANES, bn), F32)],
        compiler_params=_cparams("parallel", "arbitrary"),
        name="ffn_up",
    )(h, wg, wu, conv_w, conv_b.reshape(1, n))


def _gelu_tanh(x):
    c = math.sqrt(2.0 / math.pi)
    return 0.5 * x * (1.0 + jnp.tanh(c * (x + 0.044715 * (x * x * x))))


def _softplus(y):
    return jnp.maximum(y, 0.0) + jnp.log1p(jnp.exp(-jnp.abs(y)))


def _lru_kernel(x_ref, gt_ref, cw_ref, cb_ref, wa_ref, ba_ref, wi_ref, bi_ref, lam_ref,
                o_ref, prevx_ref, h_ref):
    t = pl.program_id(2)

    @pl.when(t == 0)
    def _():
        prevx_ref[...] = jnp.zeros_like(prevx_ref)
        h_ref[...] = jnp.zeros_like(h_ref)

    x = x_ref[...]
    rows, width = x.shape
    prev8 = prevx_ref[...]
    prevx_ref[...] = x[-SUBLANES:, :]
    cw = cw_ref[...]
    xc = cw[3:4, :] * x + cb_ref[...]
    for s in range(1, LRU_CONV_WIDTH):
        xc = xc + cw[LRU_CONV_WIDTH - 1 - s:LRU_CONV_WIDTH - s, :] * _shift_rows(x, prev8, s)

    xcb = xc.astype(BF16)
    heads = width // LRU_HEAD_DIM
    pre_a = []
    pre_i = []
    for hd in range(heads):
        xh = xcb[:, hd * LRU_HEAD_DIM:(hd + 1) * LRU_HEAD_DIM]
        pre_a.append(_dot(xh, wa_ref[hd]))
        pre_i.append(_dot(xh, wi_ref[hd]))
    r = jax.nn.sigmoid(jnp.concatenate(pre_a, axis=1) + ba_ref[...])
    gi = jax.nn.sigmoid(jnp.concatenate(pre_i, axis=1) + bi_ref[...])
    log_a = (-LRU_C) * r * _softplus(-lam_ref[...])
    a = jnp.exp(log_a)
    th = jnp.tanh(log_a)
    u = jnp.sqrt(-2.0 * th / (1.0 - th)) * (gi * xc)

    row = lax.broadcasted_iota(jnp.int32, (rows, width), 0)
    d = 1
    while d < rows:
        keep = row >= d
        a_sh = jnp.where(keep, pltpu.roll(a, d, 0), 1.0)
        u_sh = jnp.where(keep, pltpu.roll(u, d, 0), 0.0)
        u = u + a * u_sh
        a = a * a_sh
        d *= 2
    h = u + a * h_ref[...]
    h_ref[...] = h[rows - 1:rows, :]
    o_ref[...] = (h * _gelu_tanh(gt_ref[...])).astype(o_ref.dtype)


def _lru(z, conv_w, conv_b, w_a, b_a, w_i, b_i, lam, batch, seq, lru_width, bt, bw):
    m = z.shape[0]
    nt = seq // bt
    ncb = lru_width // bw
    hpb = bw // LRU_HEAD_DIM
    row = lambda b, c, t: b * nt + t
    vec = pl.BlockSpec((1, bw), lambda b, c, t: (0, c))
    return pl.pallas_call(
        _lru_kernel,
        out_shape=jax.ShapeDtypeStruct((m, lru_width), BF16),
        grid=(batch, ncb, nt),
        in_specs=[pl.BlockSpec((bt, bw), lambda b, c, t: (row(b, c, t), c)),
                  pl.BlockSpec((bt, bw), lambda b, c, t: (row(b, c, t), ncb + c)),
                  pl.BlockSpec((LRU_CONV_WIDTH, bw), lambda b, c, t: (0, c)),
                  vec,
                  pl.BlockSpec((hpb, LRU_HEAD_DIM, LRU_HEAD_DIM), lambda b, c, t: (c, 0, 0)),
                  vec,
                  pl.BlockSpec((hpb, LRU_HEAD_DIM, LRU_HEAD_DIM), lambda b, c, t: (c, 0, 0)),
                  vec, vec],
        out_specs=pl.BlockSpec((bt, bw), lambda b, c, t: (row(b, c, t), c)),
        scratch_shapes=[pltpu.VMEM((SUBLANES, bw), F32), pltpu.VMEM((1, bw), F32)],
        compiler_params=_cparams("parallel", "parallel", "arbitrary"),
        name="rg_lru",
    )(z, z, conv_w, conv_b.reshape(1, -1), w_a.astype(BF16), b_a.reshape(1, -1),
      w_i.astype(BF16), b_i.reshape(1, -1), lam.reshape(1, -1))


def _split_dot(x, w_bf16):
    hi = x.astype(BF16)
    lo = (x - hi.astype(F32)).astype(BF16)
    return _dot(hi, w_bf16) + _dot(lo, w_bf16)


def _rwkv_kernel(zr_ref, zk_ref, zv_ref, zt_ref, mur_ref, muk_ref, muv_ref, mut_ref,
                 w0_ref, w2_ref, a0_ref, a2_ref, g2_ref, kk_ref, ka_ref, rk_ref, gnw_ref, gnb_ref,
                 o_ref, pr_ref, pk_ref, pv_ref, pt_ref, state_ref, *, n_chunks):
    C = RWKV_CHUNK
    gw = o_ref.shape[1]
    pairs = gw // LANES

    @pl.when(pl.program_id(2) == 0)
    def _():
        pr_ref[...] = jnp.zeros_like(pr_ref)
        pk_ref[...] = jnp.zeros_like(pk_ref)
        pv_ref[...] = jnp.zeros_like(pv_ref)
        pt_ref[...] = jnp.zeros_like(pt_ref)
        state_ref[...] = jnp.zeros_like(state_ref)

    ri = lax.broadcasted_iota(jnp.int32, (2 * C, 2 * C), 0)
    ci = lax.broadcasted_iota(jnp.int32, (2 * C, 2 * C), 1)
    same_head = (ri >= C) == (ci >= C)
    strict = same_head & (ri > ci)
    eye = (ri == ci).astype(F32)
    rt_i = lax.broadcasted_iota(jnp.int32, (C, 2 * C), 0)
    ct_i = lax.broadcasted_iota(jnp.int32, (C, 2 * C), 1)
    incl = jnp.where(ct_i >= C, ct_i - C, ct_i) <= rt_i
    tri = (lax.broadcasted_iota(jnp.int32, (C, C), 0)
           >= lax.broadcasted_iota(jnp.int32, (C, C), 1)).astype(BF16)
    lane = lax.broadcasted_iota(jnp.int32, (C, LANES), 1)
    head0 = lane < RWKV_HEAD_DIM
    seg = ((lax.broadcasted_iota(jnp.int32, (LANES, LANES), 0) >= RWKV_HEAD_DIM)
           == (lax.broadcasted_iota(jnp.int32, (LANES, LANES), 1) >= RWKV_HEAD_DIM)).astype(BF16)
    first_row = lax.broadcasted_iota(jnp.int32, (C, 1), 0) == 0

    def stack(x):
        return jnp.concatenate([jnp.where(head0, x, 0.0), jnp.where(head0, 0.0, x)], axis=0)

    def mix(z_ref, p_ref, mu_ref, rows):
        z = z_ref[rows, :]
        rolled = pltpu.roll(z, 1, 0)
        sh = jnp.where(first_row, p_ref[...], rolled)
        p_ref[...] = rolled[0:1, :]
        return z + (sh - z) * mu_ref[...]

    def body(cidx, carry):
        rows = pl.ds(pl.multiple_of(cidx * C, C), C)
        r_all = mix(zr_ref, pr_ref, mur_ref, rows)
        k_all = mix(zk_ref, pk_ref, muk_ref, rows)
        v_all = mix(zv_ref, pv_ref, muv_ref, rows)
        tl = mix(zt_ref, pt_ref, mut_ref, rows)
        zw = jnp.tanh(tl[:, :RANK_PAD]).astype(BF16)
        za = tl[:, RANK_PAD:2 * RANK_PAD].astype(BF16)
        zg = jax.nn.sigmoid(tl[:, 2 * RANK_PAD:]).astype(BF16)
        logw_all = (-math.exp(-0.5)) * jax.nn.sigmoid(w0_ref[...] + _dot(zw, w2_ref[...]))
        a_all = jax.nn.sigmoid(a0_ref[...] + _dot(za, a2_ref[...]))
        g_all = _dot(zg, g2_ref[...])

        for p in range(pairs):
            ls = slice(p * LANES, (p + 1) * LANES)
            r, k, v, logw, a = r_all[:, ls], k_all[:, ls], v_all[:, ls], logw_all[:, ls], a_all[:, ls]
            kkp = k * kk_ref[:, ls]
            ss = _split_dot(kkp * kkp, seg)
            kk = kkp / jnp.maximum(jnp.sqrt(ss), L2_EPS)
            kp = k * (1.0 + (a - 1.0) * ka_ref[:, ls])
            b = kk * a
            bonus = _split_dot(r * kp * rk_ref[:, ls], seg) * v

            l1 = logw.astype(BF16)
            rem = logw - l1.astype(F32)
            l2 = rem.astype(BF16)
            l3 = (rem - l2.astype(F32)).astype(BF16)
            lw = _dot(tri, l1) + _dot(tri, l2) + _dot(tri, l3)
            lw_end = lw[C - 1:C, :]
            e_pos = jnp.exp(lw)
            e_neg = jnp.exp(-lw)
            e_rem = jnp.exp(lw_end - lw)
            at = -kk * jnp.exp(lw - logw)
            rt = r * e_pos
            kh = kp * e_neg
            bh = b * e_neg
            kb = kp * e_rem
            bb = b * e_rem

            at_s = stack(at).astype(BF16)
            v_s = stack(v).astype(BF16)
            lhs = jnp.concatenate([at_s, rt.astype(BF16)], axis=0)
            rhs = jnp.concatenate([stack(kh), stack(bh)], axis=0).astype(BF16)
            gram = _dot_nt(lhs, rhs)
            a_ak = jnp.where(strict, gram[:2 * C, :2 * C], 0.0)
            a_ab = jnp.where(strict, gram[:2 * C, 2 * C:], 0.0)
            a_rk = jnp.where(incl, gram[2 * C:, :2 * C], 0.0)
            a_rb = jnp.where(incl, gram[2 * C:, 2 * C:], 0.0)

            inv = eye + a_ab
            pw = a_ab.astype(BF16)
            for _ in range(int(math.log2(C)) - 1):
                pw_f = _dot(pw, pw)
                pw = pw_f.astype(BF16)
                inv = inv + _dot(pw, inv.astype(BF16))

            st = state_ref[p]
            proj = _dot_nt(lhs, st.astype(BF16))
            u_s = _dot(inv.astype(BF16), (proj[:2 * C] + _dot(a_ak.astype(BF16), v_s)).astype(BF16))
            vu = jnp.concatenate([v_s, u_s.astype(BF16)], axis=0)
            y = proj[2 * C:] + _dot(jnp.concatenate([a_rk, a_rb], axis=1).astype(BF16), vu)
            kbb = jnp.concatenate([stack(kb), stack(bb)], axis=0).astype(BF16)
            state_ref[p] = st * jnp.exp(lw_end) + _dot_tn(vu, kbb)

            mean = _split_dot(y, seg) * (1.0 / RWKV_HEAD_DIM)
            dlt = y - mean
            var = _split_dot(dlt * dlt, seg) * (1.0 / RWKV_HEAD_DIM)
            yn = dlt * lax.rsqrt(var + GN_EPS) * gnw_ref[:, ls] + gnb_ref[:, ls]
            o_ref[rows, ls] = ((yn + bonus) * g_all[:, ls]).astype(o_ref.dtype)
        return carry

    lax.fori_loop(0, n_chunks, body, 0)


def _rwkv(z, mu_r, mu_k, mu_v, mu_t, w0, w2, a0, a2, g2, k_k, k_a, r_k, gn_w, gn_b,
          batch, seq, col0, width, bt, gw):
    m = z.shape[0]
    nt = seq // bt
    ng = width // gw
    row = lambda b, g, t: b * nt + t
    cb = col0 // gw
    wb = width // gw
    zspec = lambda off: pl.BlockSpec((bt, gw), lambda b, g, t: (row(b, g, t), cb + off * wb + g))
    vec = pl.BlockSpec((1, gw), lambda b, g, t: (0, g))
    mat = lambda r: pl.BlockSpec((r, gw), lambda b, g, t: (0, g))
    tail_blk = (col0 + 3 * width) // TAIL_WIDTH
    kern = functools.partial(_rwkv_kernel, n_chunks=bt // RWKV_CHUNK)
    return pl.pallas_call(
        kern,
        out_shape=jax.ShapeDtypeStruct((m, width), BF16),
        grid=(batch, ng, nt),
        in_specs=[zspec(0), zspec(1), zspec(2),
                  pl.BlockSpec((bt, TAIL_WIDTH), lambda b, g, t: (row(b, g, t), tail_blk)),
                  vec, vec, vec,
                  pl.BlockSpec((1, TAIL_WIDTH), lambda b, g, t: (0, 0)),
                  vec, mat(RANK_PAD), vec, mat(RANK_PAD), mat(GATE_RANK),
                  vec, vec, vec, vec, vec],
        out_specs=pl.BlockSpec((bt, gw), lambda b, g, t: (row(b, g, t), g)),
        scratch_shapes=[pltpu.VMEM((1, gw), F32), pltpu.VMEM((1, gw), F32), pltpu.VMEM((1, gw), F32),
                        pltpu.VMEM((1, TAIL_WIDTH), F32),
                        pltpu.VMEM((gw // LANES, LANES, LANES), F32)],
        compiler_params=_cparams("parallel", "parallel", "arbitrary"),
        name="rwkv7",
    )(z, z, z, z, mu_r, mu_k, mu_v, mu_t, w0, w2, a0, a2, g2, k_k, k_a, r_k, gn_w, gn_b)


def _pad_rows(w, rows):
    return jnp.pad(w, ((0, rows - w.shape[0]), (0, 0)))


def _pad_cols(w, cols):
    return jnp.pad(w, ((0, 0), (0, cols - w.shape[1])))


def _tiles(m, seq):
    return dict(
        norm_bm=min(256, m),
        mm_bm=min(1024, m), inproj_bn=768, outproj_bn=512, down_bm=min(512, m), down_bn=512,
        ffn_bm=min(1024, seq), ffn_bn=256,
        lru_bt=min(256, seq), lru_bw=512,
        rwkv_bt=min(256, seq), rwkv_gw=256,
    )


def kernel(x, g_mix, w_in, conv_lru_w, conv_lru_b, lru_w_a, lru_b_a, lru_w_i, lru_b_i, lru_lambda, rwkv_mu, rwkv_w0, rwkv_w2, rwkv_a0, rwkv_a2, rwkv_g2, rwkv_k_k, rwkv_k_a, rwkv_r_k, rwkv_gn_w, rwkv_gn_b, w_out, g_ffn, w_ffn_gate, ffn_conv_w, ffn_conv_b, w_ffn_up, w_ffn_down, g_final):
    batch, seq, d_model = x.shape
    depth = w_in.shape[0]
    lru_w = lru_lambda.shape[1]
    rw = rwkv_w0.shape[1]
    m = batch * seq
    t = _tiles(m, seq)
    xf = x.reshape(m, d_model)
    c_rkv = 2 * lru_w
    c_lr = c_rkv + 3 * rw

    def row(vv):
        return vv.reshape(1, -1)

    for l in range(depth):
        wi = w_in[l]
        w_in_p = jnp.concatenate(
            [wi[:, :c_lr],
             _pad_cols(wi[:, c_lr:c_lr + DECAY_RANK], RANK_PAD),
             _pad_cols(wi[:, c_lr + DECAY_RANK:c_lr + DECAY_RANK + ICLR_RANK], RANK_PAD),
             wi[:, c_lr + DECAY_RANK + ICLR_RANK:]], axis=1).astype(BF16)
        mu = rwkv_mu[l]
        mu_t = jnp.concatenate(
            [_pad_cols(row(mu[c_lr - c_rkv:c_lr - c_rkv + DECAY_RANK]), RANK_PAD),
             _pad_cols(row(mu[c_lr - c_rkv + DECAY_RANK:c_lr - c_rkv + DECAY_RANK + ICLR_RANK]), RANK_PAD),
             row(mu[c_lr - c_rkv + DECAY_RANK + ICLR_RANK:])], axis=1)

        h = _rmsnorm(xf, g_mix[l], BF16, t["norm_bm"])
        z = _matmul(h, w_in_p, F32, t["mm_bm"], t["inproj_bn"], "inproj")
        y_lru = _lru(z, conv_lru_w[l], conv_lru_b[l], lru_w_a[l], lru_b_a[l], lru_w_i[l], lru_b_i[l],
                     lru_lambda[l], batch, seq, lru_w, t["lru_bt"], t["lru_bw"])
        y_rwkv = _rwkv(z, row(mu[:rw]), row(mu[rw:2 * rw]), row(mu[2 * rw:3 * rw]), mu_t,
                       row(rwkv_w0[l]), _pad_rows(rwkv_w2[l], RANK_PAD).astype(BF16),
                       row(rwkv_a0[l]), _pad_rows(rwkv_a2[l], RANK_PAD).astype(BF16),
                       rwkv_g2[l].astype(BF16), row(rwkv_k_k[l]), row(rwkv_k_a[l]), row(rwkv_r_k[l]),
                       row(rwkv_gn_w[l]), row(rwkv_gn_b[l]),
                       batch, seq, c_rkv, rw, t["rwkv_bt"], t["rwkv_gw"])
        xf = _outproj(y_lru, y_rwkv, w_out[l].astype(BF16), xf, t["mm_bm"], t["outproj_bn"])

        h = _rmsnorm(xf, g_ffn[l], BF16, t["norm_bm"])
        act = _ffn_up(h, w_ffn_gate[l].astype(BF16), w_ffn_up[l].astype(BF16), ffn_conv_w[l],
                      ffn_conv_b[l], seq, t["ffn_bm"], t["ffn_bn"])
        xf = _matmul_residual(act, w_ffn_down[l].astype(BF16), xf, t["down_bm"], t["down_bn"], "ffn_down")

    return _rmsnorm(xf, g_final, F32, t["norm_bm"]).reshape(batch, seq, d_model)
```

```python
import functools
import math

import jax
import jax.numpy as jnp
from jax import lax
from jax.experimental import pallas as pl
from jax.experimental.pallas import tpu as pltpu

F32 = jnp.float32
BF16 = jnp.bfloat16

V7X_VMEM_LIMIT_BYTES = 56 * 1024 * 1024
LANES = 128
SUBLANES = 8

LRU_HEAD_DIM = 128
LRU_CONV_WIDTH = 4
LRU_C = 8.0
RWKV_HEAD_DIM = 64
DECAY_RANK = 96
ICLR_RANK = 96
GATE_RANK = 256
FFN_CONV_WIDTH = 3
NORM_EPS = 1e-6
GN_EPS = 64e-5
L2_EPS = 1e-12
RANK_PAD = LANES
TAIL_WIDTH = 2 * RANK_PAD + GATE_RANK
RWKV_CHUNK = 64


def _cparams(*sem):
    return pltpu.CompilerParams(dimension_semantics=sem, vmem_limit_bytes=V7X_VMEM_LIMIT_BYTES)


def _dot(a, b):
    return jnp.dot(a, b, preferred_element_type=F32)


def _dot_nt(a, b):
    return lax.dot_general(a, b, (((1,), (1,)), ((), ())), preferred_element_type=F32)


def _dot_tn(a, b):
    return lax.dot_general(a, b, (((0,), (0,)), ((), ())), preferred_element_type=F32)


def _shift_rows(x, prev8, s):
    xe = jnp.concatenate([prev8, x], axis=0)
    return pltpu.roll(xe, s, 0)[SUBLANES:, :]


def _rmsnorm_kernel(x_ref, g_ref, o_ref):
    x = x_ref[...]
    ms = jnp.mean(x * x, axis=-1, keepdims=True)
    o_ref[...] = (x * lax.rsqrt(ms + NORM_EPS) * g_ref[...]).astype(o_ref.dtype)


def _rmsnorm(x2d, g, out_dtype, bm):
    m, d = x2d.shape
    return pl.pallas_call(
        _rmsnorm_kernel,
        out_shape=jax.ShapeDtypeStruct((m, d), out_dtype),
        grid=(m // bm,),
        in_specs=[pl.BlockSpec((bm, d), lambda i: (i, 0)),
                  pl.BlockSpec((1, d), lambda i: (0, 0))],
        out_specs=pl.BlockSpec((bm, d), lambda i: (i, 0)),
        compiler_params=_cparams("parallel"),
        name="rmsnorm",
    )(x2d, g.reshape(1, d))


def _mm_kernel(a_ref, b_ref, o_ref):
    o_ref[...] = _dot(a_ref[...], b_ref[...]).astype(o_ref.dtype)


def _matmul(a, b, out_dtype, bm, bn, name):
    m, k = a.shape
    n = b.shape[1]
    return pl.pallas_call(
        _mm_kernel,
        out_shape=jax.ShapeDtypeStruct((m, n), out_dtype),
        grid=(m // bm, n // bn),
        in_specs=[pl.BlockSpec((bm, k), lambda i, j: (i, 0)),
                  pl.BlockSpec((k, bn), lambda i, j: (0, j))],
        out_specs=pl.BlockSpec((bm, bn), lambda i, j: (i, j)),
        compiler_params=_cparams("parallel", "parallel"),
        name=name,
    )(a, b)


def _mm_res_kernel(a_ref, b_ref, r_ref, o_ref):
    o_ref[...] = r_ref[...] + _dot(a_ref[...], b_ref[...])


def _matmul_residual(a, b, res, bm, bn, name):
    m, k = a.shape
    n = b.shape[1]
    return pl.pallas_call(
        _mm_res_kernel,
        out_shape=jax.ShapeDtypeStruct((m, n), F32),
        grid=(m // bm, n // bn),
        in_specs=[pl.BlockSpec((bm, k), lambda i, j: (i, 0)),
                  pl.BlockSpec((k, bn), lambda i, j: (0, j)),
                  pl.BlockSpec((bm, bn), lambda i, j: (i, j))],
        out_specs=pl.BlockSpec((bm, bn), lambda i, j: (i, j)),
        compiler_params=_cparams("parallel", "parallel"),
        name=name,
    )(a, b, res)


def _outproj_kernel(a1_ref, a2_ref, w1_ref, w2_ref, r_ref, o_ref):
    acc = _dot(a1_ref[...], w1_ref[...]) + _dot(a2_ref[...], w2_ref[...])
    o_ref[...] = r_ref[...] + acc


def _outproj(y1, y2, w, res, bm, bn):
    m, k1 = y1.shape
    k2 = y2.shape[1]
    assert k1 == k2
    n = w.shape[1]
    return pl.pallas_call(
        _outproj_kernel,
        out_shape=jax.ShapeDtypeStruct((m, n), F32),
        grid=(m // bm, n // bn),
        in_specs=[pl.BlockSpec((bm, k1), lambda i, j: (i, 0)),
                  pl.BlockSpec((bm, k2), lambda i, j: (i, 0)),
                  pl.BlockSpec((k1, bn), lambda i, j: (0, j)),
                  pl.BlockSpec((k2, bn), lambda i, j: (1, j)),
                  pl.BlockSpec((bm, bn), lambda i, j: (i, j))],
        out_specs=pl.BlockSpec((bm, bn), lambda i, j: (i, j)),
        compiler_params=_cparams("parallel", "parallel"),
        name="outproj",
    )(y1, y2, w, w, res)


def _ffn_up_kernel(h_ref, wg_ref, wu_ref, cw_ref, cb_ref, o_ref, prev_ref, *, tiles_per_seq, n_sub):
    i = pl.program_id(1)

    @pl.when(i % tiles_per_seq == 0)
    def _():
        prev_ref[...] = jnp.zeros_like(prev_ref)

    cw = cw_ref[...]
    prev8 = prev_ref[...]
    sub = o_ref.shape[0] // n_sub
    for s in range(n_sub):
        rows = slice(s * sub, (s + 1) * sub)
        h = h_ref[rows, :]
        gate = _dot(h, wg_ref[...])
        up = _dot(h, wu_ref[...])
        conv = (cw[2:3, :] * gate + cw[1:2, :] * _shift_rows(gate, prev8, 1)
                + cw[0:1, :] * _shift_rows(gate, prev8, 2) + cb_ref[...])
        o_ref[rows, :] = (conv * jax.nn.sigmoid(conv) * up).astype(o_ref.dtype)
        prev8 = gate[-SUBLANES:, :]
    prev_ref[...] = prev8


def _ffn_up(h, wg, wu, conv_w, conv_b, seq, bm, bn, n_sub):
    m, k = h.shape
    n = wg.shape[1]
    kern = functools.partial(_ffn_up_kernel, tiles_per_seq=seq // bm, n_sub=n_sub)
    return pl.pallas_call(
        kern,
        out_shape=jax.ShapeDtypeStruct((m, n), BF16),
        grid=(n // bn, m // bm),
        in_specs=[pl.BlockSpec((bm, k), lambda j, i: (i, 0)),
                  pl.BlockSpec((k, bn), lambda j, i: (0, j)),
                  pl.BlockSpec((k, bn), lambda j, i: (0, j)),
                  pl.BlockSpec((FFN_CONV_WIDTH, bn), lambda j, i: (0, j)),
                  pl.BlockSpec((1, bn), lambda j, i: (0, j))],
        out_specs=pl.BlockSpec((bm, bn), lambda j, i: (i, j)),
        scratch_shapes=[pltpu.VMEM((SUBLANES, bn), F32)],
        compiler_params=_cparams("parallel", "arbitrary"),
        name="ffn_up",
    )(h, wg, wu, conv_w, conv_b.reshape(1, n))


def _gelu_tanh(x):
    c = math.sqrt(2.0 / math.pi)
    return 0.5 * x * (1.0 + jnp.tanh(c * (x + 0.044715 * (x * x * x))))


def _softplus(y):
    return jnp.maximum(y, 0.0) + jnp.log1p(jnp.exp(-jnp.abs(y)))


def _lru_kernel(x_ref, gt_ref, cw_ref, cb_ref, wa_ref, ba_ref, wi_ref, bi_ref, lam_ref,
                o_ref, prevx_ref, h_ref):
    t = pl.program_id(2)

    @pl.when(t == 0)
    def _():
        prevx_ref[...] = jnp.zeros_like(prevx_ref)
        h_ref[...] = jnp.zeros_like(h_ref)

    x = x_ref[...]
    rows, width = x.shape
    prev8 = prevx_ref[...]
    prevx_ref[...] = x[-SUBLANES:, :]
    cw = cw_ref[...]
    xc = cw[3:4, :] * x + cb_ref[...]
    for s in range(1, LRU_CONV_WIDTH):
        xc = xc + cw[LRU_CONV_WIDTH - 1 - s:LRU_CONV_WIDTH - s, :] * _shift_rows(x, prev8, s)

    xcb = xc.astype(BF16)
    heads = width // LRU_HEAD_DIM
    pre_a = []
    pre_i = []
    for hd in range(heads):
        xh = xcb[:, hd * LRU_HEAD_DIM:(hd + 1) * LRU_HEAD_DIM]
        pre_a.append(_dot(xh, wa_ref[hd]))
        pre_i.append(_dot(xh, wi_ref[hd]))
    r = jax.nn.sigmoid(jnp.concatenate(pre_a, axis=1) + ba_ref[...])
    gi = jax.nn.sigmoid(jnp.concatenate(pre_i, axis=1) + bi_ref[...])
    log_a = (-LRU_C) * r * _softplus(-lam_ref[...])
    a = jnp.exp(log_a)
    th = jnp.tanh(log_a)
    u = jnp.sqrt(-2.0 * th / (1.0 - th)) * (gi * xc)

    row = lax.broadcasted_iota(jnp.int32, (rows, width), 0)
    d = 1
    while d < rows:
        keep = row >= d
        a_sh = jnp.where(keep, pltpu.roll(a, d, 0), 1.0)
        u_sh = jnp.where(keep, pltpu.roll(u, d, 0), 0.0)
        u = u + a * u_sh
        a = a * a_sh
        d *= 2
    h = u + a * h_ref[...]
    h_ref[...] = h[rows - 1:rows, :]
    o_ref[...] = (h * _gelu_tanh(gt_ref[...])).astype(o_ref.dtype)


def _lru(z, conv_w, conv_b, w_a, b_a, w_i, b_i, lam, batch, seq, lru_width, bt, bw):
    m = z.shape[0]
    nt = seq // bt
    ncb = lru_width // bw
    hpb = bw // LRU_HEAD_DIM
    row = lambda b, c, t: b * nt + t
    vec = pl.BlockSpec((1, bw), lambda b, c, t: (0, c))
    return pl.pallas_call(
        _lru_kernel,
        out_shape=jax.ShapeDtypeStruct((m, lru_width), BF16),
        grid=(batch, ncb, nt),
        in_specs=[pl.BlockSpec((bt, bw), lambda b, c, t: (row(b, c, t), c)),
                  pl.BlockSpec((bt, bw), lambda b, c, t: (row(b, c, t), ncb + c)),
                  pl.BlockSpec((LRU_CONV_WIDTH, bw), lambda b, c, t: (0, c)),
                  vec,
                  pl.BlockSpec((hpb, LRU_HEAD_DIM, LRU_HEAD_DIM), lambda b, c, t: (c, 0, 0)),
                  vec,
                  pl.BlockSpec((hpb, LRU_HEAD_DIM, LRU_HEAD_DIM), lambda b, c, t: (c, 0, 0)),
                  vec, vec],
        out_specs=pl.BlockSpec((bt, bw), lambda b, c, t: (row(b, c, t), c)),
        scratch_shapes=[pltpu.VMEM((SUBLANES, bw), F32), pltpu.VMEM((1, bw), F32)],
        compiler_params=_cparams("parallel", "parallel", "arbitrary"),
        name="rg_lru",
    )(z, z, conv_w, conv_b.reshape(1, -1), w_a.astype(BF16), b_a.reshape(1, -1),
      w_i.astype(BF16), b_i.reshape(1, -1), lam.reshape(1, -1))


def _split_bf16(x):
    hi = x.astype(BF16)
    lo = (x - hi.astype(F32)).astype(BF16)
    return hi, lo


def _rwkv_kernel(zr_ref, zk_ref, zv_ref, zt_ref, mur_ref, muk_ref, muv_ref, mut_ref,
                 w0_ref, w2_ref, a0_ref, a2_ref, g2_ref, kk_ref, ka_ref, rk_ref, gnw_ref, gnb_ref,
                 o_ref, pr_ref, pk_ref, pv_ref, pt_ref, state_ref, *, n_chunks):
    C = RWKV_CHUNK
    C2 = 2 * C
    gw = o_ref.shape[1]
    pairs = gw // LANES
    seg_w = 2 * LANES

    @pl.when(pl.program_id(2) == 0)
    def _():
        pr_ref[...] = jnp.zeros_like(pr_ref)
        pk_ref[...] = jnp.zeros_like(pk_ref)
        pv_ref[...] = jnp.zeros_like(pv_ref)
        pt_ref[...] = jnp.zeros_like(pt_ref)
        state_ref[...] = jnp.zeros_like(state_ref)

    ri = lax.broadcasted_iota(jnp.int32, (C2, C2), 0)
    ci = lax.broadcasted_iota(jnp.int32, (C2, C2), 1)
    strict = ((ri >= C) == (ci >= C)) & (ri > ci)
    rt_i = lax.broadcasted_iota(jnp.int32, (C, C2), 0)
    ct_i = lax.broadcasted_iota(jnp.int32, (C, C2), 1)
    incl = jnp.where(ct_i >= C, ct_i - C, ct_i) <= rt_i
    tri3 = (lax.broadcasted_iota(jnp.int32, (C, 3 * C), 0)
            >= lax.broadcasted_iota(jnp.int32, (C, 3 * C), 1) % C).astype(BF16)
    head0 = lax.broadcasted_iota(jnp.int32, (C, LANES), 1) < RWKV_HEAD_DIM
    seg = ((lax.broadcasted_iota(jnp.int32, (seg_w, seg_w), 0) // RWKV_HEAD_DIM)
           == (lax.broadcasted_iota(jnp.int32, (seg_w, seg_w), 1) // RWKV_HEAD_DIM)).astype(BF16)
    first_row = lax.broadcasted_iota(jnp.int32, (C, 1), 0) == 0
    P = range(pairs)

    def pair(x, p):
        return x[:, p * LANES:(p + 1) * LANES]

    def stack(x):
        return jnp.concatenate([jnp.where(head0, x, 0.0), jnp.where(head0, 0.0, x)], axis=0)

    def head_sum(x):
        hi, lo = _split_bf16(x)
        cols = []
        for q in range(gw // seg_w):
            s = slice(q * seg_w, (q + 1) * seg_w)
            cols.append(_dot(hi[:, s], seg) + _dot(lo[:, s], seg))
        return jnp.concatenate(cols, axis=1)

    def mix(z_ref, p_ref, mu_ref, rows):
        z = z_ref[rows, :]
        rolled = pltpu.roll(z, 1, 0)
        sh = jnp.where(first_row, p_ref[...], rolled)
        p_ref[...] = rolled[0:1, :]
        return z + (sh - z) * mu_ref[...]

    def body(cidx, carry):
        rows = pl.ds(pl.multiple_of(cidx * C, C), C)
        r = mix(zr_ref, pr_ref, mur_ref, rows)
        k = mix(zk_ref, pk_ref, muk_ref, rows)
        v = mix(zv_ref, pv_ref, muv_ref, rows)
        tl = mix(zt_ref, pt_ref, mut_ref, rows)
        zw = jnp.tanh(tl[:, :RANK_PAD]).astype(BF16)
        za = tl[:, RANK_PAD:2 * RANK_PAD].astype(BF16)
        zg = jax.nn.sigmoid(tl[:, 2 * RANK_PAD:]).astype(BF16)
        logw = (-math.exp(-0.5)) * jax.nn.sigmoid(w0_ref[...] + _dot(zw, w2_ref[...]))
        a = jax.nn.sigmoid(a0_ref[...] + _dot(za, a2_ref[...]))
        g = _dot(zg, g2_ref[...])

        kkp = k * kk_ref[...]
        kk = kkp / jnp.maximum(jnp.sqrt(head_sum(kkp * kkp)), L2_EPS)
        kp = k * (1.0 + (a - 1.0) * ka_ref[...])
        b = kk * a
        bonus = head_sum(r * kp * rk_ref[...]) * v

        l1 = logw.astype(BF16)
        rem = logw - l1.astype(F32)
        l2 = rem.astype(BF16)
        l3 = (rem - l2.astype(F32)).astype(BF16)
        lw = _dot(tri3, jnp.concatenate([l1, l2, l3], axis=0))
        lw_end = lw[C - 1:C, :]
        e_neg = jnp.exp(-lw)
        e_rem = jnp.exp(lw_end - lw)
        e_end = jnp.exp(lw_end)
        at = -kk * jnp.exp(lw - logw)
        rt = (r * jnp.exp(lw)).astype(BF16)
        kh = kp * e_neg
        bh = b * e_neg
        kb = kp * e_rem
        bb = b * e_rem

        at_s = [stack(pair(at, p)).astype(BF16) for p in P]
        v_s = [stack(pair(v, p)).astype(BF16) for p in P]
        lhs = [jnp.concatenate([at_s[p], pair(rt, p)], axis=0) for p in P]
        rhs = [jnp.concatenate([stack(pair(kh, p)), stack(pair(bh, p))], axis=0).astype(BF16) for p in P]
        gram = [_dot_nt(lhs[p], rhs[p]) for p in P]
        st = [state_ref[p] for p in P]
        proj = [_dot_nt(lhs[p], st[p].astype(BF16)) for p in P]
        a_ak = [jnp.where(strict, gram[p][:C2, :C2], 0.0).astype(BF16) for p in P]
        q = [jnp.where(strict, gram[p][:C2, C2:], 0.0).astype(BF16) for p in P]
        a_r = [jnp.concatenate([jnp.where(incl, gram[p][C2:, :C2], 0.0),
                                jnp.where(incl, gram[p][C2:, C2:], 0.0)], axis=1).astype(BF16) for p in P]

        u = [proj[p][:C2] + _dot(a_ak[p], v_s[p]) for p in P]
        steps = int(math.log2(C))
        for j in range(steps):
            if j < steps - 1:
                x = [_dot(q[p], jnp.concatenate([q[p], u[p].astype(BF16)], axis=1)) for p in P]
                u = [u[p] + x[p][:, C2:] for p in P]
                q = [x[p][:, :C2].astype(BF16) for p in P]
            else:
                u = [u[p] + _dot(q[p], u[p].astype(BF16)) for p in P]

        vu = [jnp.concatenate([v_s[p], u[p].astype(BF16)], axis=0) for p in P]
        y = jnp.concatenate([proj[p][C2:] + _dot(a_r[p], vu[p]) for p in P], axis=1)
        kbb = [jnp.concatenate([stack(pair(kb, p)), stack(pair(bb, p))], axis=0).astype(BF16) for p in P]
        for p in P:
            state_ref[p] = st[p] * pair(e_end, p) + _dot_tn(vu[p], kbb[p])

        mean = head_sum(y) * (1.0 / RWKV_HEAD_DIM)
        dlt = y - mean
        var = head_sum(dlt * dlt) * (1.0 / RWKV_HEAD_DIM)
        yn = dlt * lax.rsqrt(var + GN_EPS) * gnw_ref[...] + gnb_ref[...]
        o_ref[rows, :] = ((yn + bonus) * g).astype(o_ref.dtype)
        return carry

    lax.fori_loop(0, n_chunks, body, 0)


def _rwkv(z, mu_r, mu_k, mu_v, mu_t, w0, w2, a0, a2, g2, k_k, k_a, r_k, gn_w, gn_b,
          batch, seq, col0, width, bt, gw):
    m = z.shape[0]
    nt = seq // bt
    ng = width // gw
    row = lambda b, g, t: b * nt + t
    cb = col0 // gw
    wb = width // gw
    zspec = lambda off: pl.BlockSpec((bt, gw), lambda b, g, t: (row(b, g, t), cb + off * wb + g))
    vec = pl.BlockSpec((1, gw), lambda b, g, t: (0, g))
    mat = lambda r: pl.BlockSpec((r, gw), lambda b, g, t: (0, g))
    tail_blk = (col0 + 3 * width) // TAIL_WIDTH
    kern = functools.partial(_rwkv_kernel, n_chunks=bt // RWKV_CHUNK)
    return pl.pallas_call(
        kern,
        out_shape=jax.ShapeDtypeStruct((m, width), BF16),
        grid=(batch, ng, nt),
        in_specs=[zspec(0), zspec(1), zspec(2),
                  pl.BlockSpec((bt, TAIL_WIDTH), lambda b, g, t: (row(b, g, t), tail_blk)),
                  vec, vec, vec,
                  pl.BlockSpec((1, TAIL_WIDTH), lambda b, g, t: (0, 0)),
                  vec, mat(RANK_PAD), vec, mat(RANK_PAD), mat(GATE_RANK),
                  vec, vec, vec, vec, vec],
        out_specs=pl.BlockSpec((bt, gw), lambda b, g, t: (row(b, g, t), g)),
        scratch_shapes=[pltpu.VMEM((1, gw), F32), pltpu.VMEM((1, gw), F32), pltpu.VMEM((1, gw), F32),
                        pltpu.VMEM((1, TAIL_WIDTH), F32),
                        pltpu.VMEM((gw // LANES, LANES, LANES), F32)],
        compiler_params=_cparams("parallel", "parallel", "arbitrary"),
        name="rwkv7",
    )(z, z, z, z, mu_r, mu_k, mu_v, mu_t, w0, w2, a0, a2, g2, k_k, k_a, r_k, gn_w, gn_b)


def _pad_rows(w, rows):
    return jnp.pad(w, ((0, rows - w.shape[0]), (0, 0)))


def _pad_cols(w, cols):
    return jnp.pad(w, ((0, 0), (0, cols - w.shape[1])))


def _tiles(m, seq):
    return dict(
        norm_bm=min(256, m),
        mm_bm=min(1024, m), inproj_bn=768, outproj_bn=512, down_bm=min(512, m), down_bn=512,
        ffn_bm=min(2048, seq), ffn_bn=256, ffn_sub=2,
        lru_bt=min(256, seq), lru_bw=512,
        rwkv_bt=min(256, seq), rwkv_gw=2048,
    )


def kernel(x, g_mix, w_in, conv_lru_w, conv_lru_b, lru_w_a, lru_b_a, lru_w_i, lru_b_i, lru_lambda, rwkv_mu, rwkv_w0, rwkv_w2, rwkv_a0, rwkv_a2, rwkv_g2, rwkv_k_k, rwkv_k_a, rwkv_r_k, rwkv_gn_w, rwkv_gn_b, w_out, g_ffn, w_ffn_gate, ffn_conv_w, ffn_conv_b, w_ffn_up, w_ffn_down, g_final):
    batch, seq, d_model = x.shape
    depth = w_in.shape[0]
    lru_w = lru_lambda.shape[1]
    rw = rwkv_w0.shape[1]
    m = batch * seq
    t = _tiles(m, seq)
    xf = x.reshape(m, d_model)
    c_rkv = 2 * lru_w
    c_lr = c_rkv + 3 * rw

    def row(vv):
        return vv.reshape(1, -1)

    for l in range(depth):
        wi = w_in[l]
        w_in_p = jnp.concatenate(
            [wi[:, :c_lr],
             _pad_cols(wi[:, c_lr:c_lr + DECAY_RANK], RANK_PAD),
             _pad_cols(wi[:, c_lr + DECAY_RANK:c_lr + DECAY_RANK + ICLR_RANK], RANK_PAD),
             wi[:, c_lr + DECAY_RANK + ICLR_RANK:]], axis=1).astype(BF16)
        mu = rwkv_mu[l]
        mu_t = jnp.concatenate(
            [_pad_cols(row(mu[c_lr - c_rkv:c_lr - c_rkv + DECAY_RANK]), RANK_PAD),
             _pad_cols(row(mu[c_lr - c_rkv + DECAY_RANK:c_lr - c_rkv + DECAY_RANK + ICLR_RANK]), RANK_PAD),
             row(mu[c_lr - c_rkv + DECAY_RANK + ICLR_RANK:])], axis=1)

        h = _rmsnorm(xf, g_mix[l], BF16, t["norm_bm"])
        z = _matmul(h, w_in_p, F32, t["mm_bm"], t["inproj_bn"], "inproj")
        y_lru = _lru(z, conv_lru_w[l], conv_lru_b[l], lru_w_a[l], lru_b_a[l], lru_w_i[l], lru_b_i[l],
                     lru_lambda[l], batch, seq, lru_w, t["lru_bt"], t["lru_bw"])
        y_rwkv = _rwkv(z, row(mu[:rw]), row(mu[rw:2 * rw]), row(mu[2 * rw:3 * rw]), mu_t,
                       row(rwkv_w0[l]), _pad_rows(rwkv_w2[l], RANK_PAD).astype(BF16),
                       row(rwkv_a0[l]), _pad_rows(rwkv_a2[l], RANK_PAD).astype(BF16),
                       rwkv_g2[l].astype(BF16), row(rwkv_k_k[l]), row(rwkv_k_a[l]), row(rwkv_r_k[l]),
                       row(rwkv_gn_w[l]), row(rwkv_gn_b[l]),
                       batch, seq, c_rkv, rw, t["rwkv_bt"], t["rwkv_gw"])
        xf = _outproj(y_lru, y_rwkv, w_out[l].astype(BF16), xf, t["mm_bm"], t["outproj_bn"])

        h = _rmsnorm(xf, g_ffn[l], BF16, t["norm_bm"])
        act = _ffn_up(h, w_ffn_gate[l].astype(BF16), w_ffn_up[l].astype(BF16), ffn_conv_w[l],
                      ffn_conv_b[l], seq, t["ffn_bm"], t["ffn_bn"], t["ffn_sub"])
        xf = _matmul_residual(act, w_ffn_down[l].astype(BF16), xf, t["down_bm"], t["down_bn"], "ffn_down")

    return _rmsnorm(xf, g_final, F32, t["norm_bm"]).reshape(batch, seq, d_model)
```

```python
import functools
import math

import jax
import jax.numpy as jnp
from jax import lax
from jax.experimental import pallas as pl
from jax.experimental.pallas import tpu as pltpu

F32 = jnp.float32
BF16 = jnp.bfloat16

V7X_VMEM_LIMIT_BYTES = 56 * 1024 * 1024
LANES = 128
SUBLANES = 8

LRU_HEAD_DIM = 128
LRU_CONV_WIDTH = 4
LRU_C = 8.0
RWKV_HEAD_DIM = 64
DECAY_RANK = 96
ICLR_RANK = 96
GATE_RANK = 256
FFN_CONV_WIDTH = 3
NORM_EPS = 1e-6
GN_EPS = 64e-5
L2_EPS = 1e-12
RANK_PAD = LANES
TAIL_WIDTH = 2 * RANK_PAD + GATE_RANK
RWKV_CHUNK = 64


def _cparams(*sem):
    return pltpu.CompilerParams(dimension_semantics=sem, vmem_limit_bytes=V7X_VMEM_LIMIT_BYTES)


def _dot(a, b):
    return jnp.dot(a, b, preferred_element_type=F32)


def _dot_nt(a, b):
    return lax.dot_general(a, b, (((1,), (1,)), ((), ())), preferred_element_type=F32)


def _dot_tn(a, b):
    return lax.dot_general(a, b, (((0,), (0,)), ((), ())), preferred_element_type=F32)


def _sigmoid(x):
    return 0.5 * jnp.tanh(0.5 * x) + 0.5


def _shift_rows(x, prev8, s):
    xe = jnp.concatenate([prev8, x], axis=0)
    return pltpu.roll(xe, s, 0)[SUBLANES:, :]


def _rmsnorm_kernel(x_ref, g_ref, o_ref):
    x = x_ref[...]
    ms = jnp.mean(x * x, axis=-1, keepdims=True)
    o_ref[...] = (x * lax.rsqrt(ms + NORM_EPS) * g_ref[...]).astype(o_ref.dtype)


def _rmsnorm(x2d, g, out_dtype, bm):
    m, d = x2d.shape
    return pl.pallas_call(
        _rmsnorm_kernel,
        out_shape=jax.ShapeDtypeStruct((m, d), out_dtype),
        grid=(m // bm,),
        in_specs=[pl.BlockSpec((bm, d), lambda i: (i, 0)),
                  pl.BlockSpec((1, d), lambda i: (0, 0))],
        out_specs=pl.BlockSpec((bm, d), lambda i: (i, 0)),
        compiler_params=_cparams("parallel"),
        name="rmsnorm",
    )(x2d, g.reshape(1, d))


def _mm_kernel(a_ref, b_ref, o_ref):
    o_ref[...] = _dot(a_ref[...], b_ref[...]).astype(o_ref.dtype)


def _matmul(a, b, out_dtype, bm, bn, name):
    m, k = a.shape
    n = b.shape[1]
    return pl.pallas_call(
        _mm_kernel,
        out_shape=jax.ShapeDtypeStruct((m, n), out_dtype),
        grid=(m // bm, n // bn),
        in_specs=[pl.BlockSpec((bm, k), lambda i, j: (i, 0)),
                  pl.BlockSpec((k, bn), lambda i, j: (0, j))],
        out_specs=pl.BlockSpec((bm, bn), lambda i, j: (i, j)),
        compiler_params=_cparams("parallel", "parallel"),
        name=name,
    )(a, b)


def _mm_wcast_kernel(a_ref, w_ref, o_ref, wb_ref):
    @pl.when(pl.program_id(1) == 0)
    def _():
        wb_ref[...] = w_ref[...].astype(BF16)

    o_ref[...] = _dot(a_ref[...], wb_ref[...]).astype(o_ref.dtype)


def _matmul_wcast(a, w_f32, n, out_dtype, bm, bn, name):
    m, k = a.shape
    return pl.pallas_call(
        _mm_wcast_kernel,
        out_shape=jax.ShapeDtypeStruct((m, n), out_dtype),
        grid=(n // bn, m // bm),
        in_specs=[pl.BlockSpec((bm, k), lambda j, i: (i, 0)),
                  pl.BlockSpec((k, bn), lambda j, i: (0, j))],
        out_specs=pl.BlockSpec((bm, bn), lambda j, i: (i, j)),
        scratch_shapes=[pltpu.VMEM((k, bn), BF16)],
        compiler_params=_cparams("parallel", "arbitrary"),
        name=name,
    )(a, w_f32)


def _mm_res_kernel(a_ref, b_ref, r_ref, o_ref):
    o_ref[...] = r_ref[...] + _dot(a_ref[...], b_ref[...])


def _matmul_residual(a, b, res, bm, bn, name):
    m, k = a.shape
    n = b.shape[1]
    return pl.pallas_call(
        _mm_res_kernel,
        out_shape=jax.ShapeDtypeStruct((m, n), F32),
        grid=(m // bm, n // bn),
        in_specs=[pl.BlockSpec((bm, k), lambda i, j: (i, 0)),
                  pl.BlockSpec((k, bn), lambda i, j: (0, j)),
                  pl.BlockSpec((bm, bn), lambda i, j: (i, j))],
        out_specs=pl.BlockSpec((bm, bn), lambda i, j: (i, j)),
        compiler_params=_cparams("parallel", "parallel"),
        name=name,
    )(a, b, res)


def _outproj_kernel(a1_ref, a2_ref, w1_ref, w2_ref, r_ref, o_ref):
    acc = _dot(a1_ref[...], w1_ref[...]) + _dot(a2_ref[...], w2_ref[...])
    o_ref[...] = r_ref[...] + acc


def _outproj(y1, y2, w, res, bm, bn):
    m, k1 = y1.shape
    k2 = y2.shape[1]
    assert k1 == k2
    n = w.shape[1]
    return pl.pallas_call(
        _outproj_kernel,
        out_shape=jax.ShapeDtypeStruct((m, n), F32),
        grid=(m // bm, n // bn),
        in_specs=[pl.BlockSpec((bm, k1), lambda i, j: (i, 0)),
                  pl.BlockSpec((bm, k2), lambda i, j: (i, 0)),
                  pl.BlockSpec((k1, bn), lambda i, j: (0, j)),
                  pl.BlockSpec((k2, bn), lambda i, j: (1, j)),
                  pl.BlockSpec((bm, bn), lambda i, j: (i, j))],
        out_specs=pl.BlockSpec((bm, bn), lambda i, j: (i, j)),
        compiler_params=_cparams("parallel", "parallel"),
        name="outproj",
    )(y1, y2, w, w, res)


def _ffn_up_kernel(h_ref, wg_ref, wu_ref, cw_ref, cb_ref, o_ref, prev_ref, wgb_ref, wub_ref,
                   *, tiles_per_seq, n_sub):
    i = pl.program_id(1)

    @pl.when(i == 0)
    def _():
        wgb_ref[...] = wg_ref[...].astype(BF16)
        wub_ref[...] = wu_ref[...].astype(BF16)

    @pl.when(i % tiles_per_seq == 0)
    def _():
        prev_ref[...] = jnp.zeros_like(prev_ref)

    cw = cw_ref[...]
    prev8 = prev_ref[...]
    sub = o_ref.shape[0] // n_sub
    for s in range(n_sub):
        rows = slice(s * sub, (s + 1) * sub)
        h = h_ref[rows, :]
        gate = _dot(h, wgb_ref[...])
        up = _dot(h, wub_ref[...])
        conv = (cw[2:3, :] * gate + cw[1:2, :] * _shift_rows(gate, prev8, 1)
                + cw[0:1, :] * _shift_rows(gate, prev8, 2) + cb_ref[...])
        o_ref[rows, :] = (conv * _sigmoid(conv) * up).astype(o_ref.dtype)
        prev8 = gate[-SUBLANES:, :]
    prev_ref[...] = prev8


def _ffn_up(h, wg, wu, conv_w, conv_b, seq, bm, bn, n_sub):
    m, k = h.shape
    n = wg.shape[1]
    kern = functools.partial(_ffn_up_kernel, tiles_per_seq=seq // bm, n_sub=n_sub)
    return pl.pallas_call(
        kern,
        out_shape=jax.ShapeDtypeStruct((m, n), BF16),
        grid=(n // bn, m // bm),
        in_specs=[pl.BlockSpec((bm, k), lambda j, i: (i, 0)),
                  pl.BlockSpec((k, bn), lambda j, i: (0, j)),
                  pl.BlockSpec((k, bn), lambda j, i: (0, j)),
                  pl.BlockSpec((FFN_CONV_WIDTH, bn), lambda j, i: (0, j)),
                  pl.BlockSpec((1, bn), lambda j, i: (0, j))],
        out_specs=pl.BlockSpec((bm, bn), lambda j, i: (i, j)),
        scratch_shapes=[pltpu.VMEM((SUBLANES, bn), F32), pltpu.VMEM((k, bn), BF16),
                        pltpu.VMEM((k, bn), BF16)],
        compiler_params=_cparams("parallel", "arbitrary"),
        name="ffn_up",
    )(h, wg, wu, conv_w, conv_b.reshape(1, n))


def _gelu_tanh(x):
    c = math.sqrt(2.0 / math.pi)
    return 0.5 * x * (1.0 + jnp.tanh(c * (x + 0.044715 * (x * x * x))))


def _softplus(y):
    return jnp.maximum(y, 0.0) + jnp.log1p(jnp.exp(-jnp.abs(y)))


def _lru_kernel(x_ref, gt_ref, cw_ref, cb_ref, wa_ref, ba_ref, wi_ref, bi_ref, lam_ref,
                o_ref, prevx_ref, h_ref):
    t = pl.program_id(2)

    @pl.when(t == 0)
    def _():
        prevx_ref[...] = jnp.zeros_like(prevx_ref)
        h_ref[...] = jnp.zeros_like(h_ref)

    x = x_ref[...]
    rows, width = x.shape
    prev8 = prevx_ref[...]
    prevx_ref[...] = x[-SUBLANES:, :]
    cw = cw_ref[...]
    xc = cw[3:4, :] * x + cb_ref[...]
    for s in range(1, LRU_CONV_WIDTH):
        xc = xc + cw[LRU_CONV_WIDTH - 1 - s:LRU_CONV_WIDTH - s, :] * _shift_rows(x, prev8, s)

    xcb = xc.astype(BF16)
    heads = width // LRU_HEAD_DIM
    pre_a = []
    pre_i = []
    for hd in range(heads):
        xh = xcb[:, hd * LRU_HEAD_DIM:(hd + 1) * LRU_HEAD_DIM]
        pre_a.append(_dot(xh, wa_ref[hd]))
        pre_i.append(_dot(xh, wi_ref[hd]))
    r = _sigmoid(jnp.concatenate(pre_a, axis=1) + ba_ref[...])
    gi = _sigmoid(jnp.concatenate(pre_i, axis=1) + bi_ref[...])
    log_a = (-LRU_C) * r * _softplus(-lam_ref[...])
    a = jnp.exp(log_a)
    th = jnp.tanh(log_a)
    num = -2.0 * th
    mult = jnp.where(num > 0.0, num * lax.rsqrt(num * (1.0 - th)), 0.0)
    u = mult * (gi * xc)

    n_groups = rows // SUBLANES
    a = a.reshape(n_groups, SUBLANES, width)
    u = u.reshape(n_groups, SUBLANES, width)
    sub = lax.broadcasted_iota(jnp.int32, (n_groups, SUBLANES, width), 1)
    d = 1
    while d < SUBLANES:
        keep = sub >= d
        a_sh = jnp.where(keep, pltpu.roll(a, d, 1), 1.0)
        u_sh = jnp.where(keep, pltpu.roll(u, d, 1), 0.0)
        u = u + a * u_sh
        a = a * a_sh
        d *= 2
    carry = h_ref[...]
    groups = []
    for gi in range(n_groups):
        hg = u[gi] + a[gi] * carry
        groups.append(hg)
        carry = hg[SUBLANES - 1:SUBLANES, :]
    h = jnp.concatenate(groups, axis=0)
    h_ref[...] = carry
    o_ref[...] = (h * _gelu_tanh(gt_ref[...])).astype(o_ref.dtype)


def _lru(z, conv_w, conv_b, w_a, b_a, w_i, b_i, lam, batch, seq, lru_width, bt, bw):
    m = z.shape[0]
    nt = seq // bt
    ncb = lru_width // bw
    hpb = bw // LRU_HEAD_DIM
    row = lambda b, c, t: b * nt + t
    vec = pl.BlockSpec((1, bw), lambda b, c, t: (0, c))
    return pl.pallas_call(
        _lru_kernel,
        out_shape=jax.ShapeDtypeStruct((m, lru_width), BF16),
        grid=(batch, ncb, nt),
        in_specs=[pl.BlockSpec((bt, bw), lambda b, c, t: (row(b, c, t), c)),
                  pl.BlockSpec((bt, bw), lambda b, c, t: (row(b, c, t), ncb + c)),
                  pl.BlockSpec((LRU_CONV_WIDTH, bw), lambda b, c, t: (0, c)),
                  vec,
                  pl.BlockSpec((hpb, LRU_HEAD_DIM, LRU_HEAD_DIM), lambda b, c, t: (c, 0, 0)),
                  vec,
                  pl.BlockSpec((hpb, LRU_HEAD_DIM, LRU_HEAD_DIM), lambda b, c, t: (c, 0, 0)),
                  vec, vec],
        out_specs=pl.BlockSpec((bt, bw), lambda b, c, t: (row(b, c, t), c)),
        scratch_shapes=[pltpu.VMEM((SUBLANES, bw), F32), pltpu.VMEM((1, bw), F32)],
        compiler_params=_cparams("parallel", "parallel", "arbitrary"),
        name="rg_lru",
    )(z, z, conv_w, conv_b.reshape(1, -1), w_a.astype(BF16), b_a.reshape(1, -1),
      w_i.astype(BF16), b_i.reshape(1, -1), lam.reshape(1, -1))


def _split_bf16(x):
    hi = x.astype(BF16)
    lo = (x - hi.astype(F32)).astype(BF16)
    return hi, lo


def _rwkv_kernel(zr_ref, zk_ref, zv_ref, zt_ref, mur_ref, muk_ref, muv_ref, mut_ref,
                 w0_ref, w2_ref, a0_ref, a2_ref, g2_ref, kk_ref, ka_ref, rk_ref, gnw_ref, gnb_ref,
                 o_ref, pr_ref, pk_ref, pv_ref, pt_ref, state_ref, *, n_chunks):
    C = RWKV_CHUNK
    C2 = 2 * C
    gw = o_ref.shape[1]
    pairs = gw // LANES
    seg_w = 2 * LANES

    @pl.when(pl.program_id(2) == 0)
    def _():
        pr_ref[...] = jnp.zeros_like(pr_ref)
        pk_ref[...] = jnp.zeros_like(pk_ref)
        pv_ref[...] = jnp.zeros_like(pv_ref)
        pt_ref[...] = jnp.zeros_like(pt_ref)
        state_ref[...] = jnp.zeros_like(state_ref)

    ri = lax.broadcasted_iota(jnp.int32, (C2, C2), 0)
    ci = lax.broadcasted_iota(jnp.int32, (C2, C2), 1)
    strict = ((ri >= C) == (ci >= C)) & (ri > ci)
    rt_i = lax.broadcasted_iota(jnp.int32, (C, C2), 0)
    ct_i = lax.broadcasted_iota(jnp.int32, (C, C2), 1)
    incl = jnp.where(ct_i >= C, ct_i - C, ct_i) <= rt_i
    tri3 = (lax.broadcasted_iota(jnp.int32, (C, 3 * C), 0)
            >= lax.broadcasted_iota(jnp.int32, (C, 3 * C), 1) % C).astype(BF16)
    head0 = lax.broadcasted_iota(jnp.int32, (C, LANES), 1) < RWKV_HEAD_DIM
    seg = ((lax.broadcasted_iota(jnp.int32, (seg_w, seg_w), 0) // RWKV_HEAD_DIM)
           == (lax.broadcasted_iota(jnp.int32, (seg_w, seg_w), 1) // RWKV_HEAD_DIM)).astype(BF16)
    first_row = lax.broadcasted_iota(jnp.int32, (C, 1), 0) == 0
    P = range(pairs)

    def pair(x, p):
        return x[:, p * LANES:(p + 1) * LANES]

    def stack(x):
        return jnp.concatenate([jnp.where(head0, x, 0.0), jnp.where(head0, 0.0, x)], axis=0)

    def head_sum(x):
        xb = x.astype(BF16)
        return jnp.concatenate(
            [_dot(xb[:, q * seg_w:(q + 1) * seg_w], seg) for q in range(gw // seg_w)], axis=1)

    def mix(z_ref, p_ref, mu_ref, rows):
        z = z_ref[rows, :]
        rolled = pltpu.roll(z, 1, 0)
        sh = jnp.where(first_row, p_ref[...], rolled)
        p_ref[...] = rolled[0:1, :]
        return z + (sh - z) * mu_ref[...]

    def body(cidx, carry):
        rows = pl.ds(pl.multiple_of(cidx * C, C), C)
        r = mix(zr_ref, pr_ref, mur_ref, rows)
        k = mix(zk_ref, pk_ref, muk_ref, rows)
        v = mix(zv_ref, pv_ref, muv_ref, rows)
        tl = mix(zt_ref, pt_ref, mut_ref, rows)
        zw = jnp.tanh(tl[:, :RANK_PAD]).astype(BF16)
        za = tl[:, RANK_PAD:2 * RANK_PAD].astype(BF16)
        zg = _sigmoid(tl[:, 2 * RANK_PAD:]).astype(BF16)
        logw = (-math.exp(-0.5)) * _sigmoid(w0_ref[...] + _dot(zw, w2_ref[...]))
        a = _sigmoid(a0_ref[...] + _dot(za, a2_ref[...]))
        g = _dot(zg, g2_ref[...])

        kkp = k * kk_ref[...]
        kk = kkp * jnp.minimum(lax.rsqrt(head_sum(kkp * kkp)), 1.0 / L2_EPS)
        kp = k * (1.0 + (a - 1.0) * ka_ref[...])
        b = kk * a
        bonus = head_sum(r * kp * rk_ref[...]) * v

        l1 = logw.astype(BF16)
        rem = logw - l1.astype(F32)
        l2 = rem.astype(BF16)
        l3 = (rem - l2.astype(F32)).astype(BF16)
        lw = _dot(tri3, jnp.concatenate([l1, l2, l3], axis=0))
        lw_end = lw[C - 1:C, :]
        e_neg = jnp.exp(-lw)
        e_rem = jnp.exp(lw_end - lw)
        e_end = jnp.exp(lw_end)
        at = -kk * jnp.exp(lw - logw)
        rt = (r * jnp.exp(lw)).astype(BF16)
        kh = kp * e_neg
        bh = b * e_neg
        kb = kp * e_rem
        bb = b * e_rem

        at_s = [stack(pair(at, p)).astype(BF16) for p in P]
        v_s = [stack(pair(v, p)).astype(BF16) for p in P]
        lhs = [jnp.concatenate([at_s[p], pair(rt, p)], axis=0) for p in P]
        rhs = [jnp.concatenate([stack(pair(kh, p)), stack(pair(bh, p))], axis=0).astype(BF16) for p in P]
        gram = [_dot_nt(lhs[p], rhs[p]) for p in P]
        st = [state_ref[p] for p in P]
        proj = [_dot_nt(lhs[p], st[p].astype(BF16)) for p in P]
        a_ak = [jnp.where(strict, gram[p][:C2, :C2], 0.0).astype(BF16) for p in P]
        q = [jnp.where(strict, gram[p][:C2, C2:], 0.0).astype(BF16) for p in P]
        a_r = [jnp.concatenate([jnp.where(incl, gram[p][C2:, :C2], 0.0),
                                jnp.where(incl, gram[p][C2:, C2:], 0.0)], axis=1).astype(BF16) for p in P]

        u = [proj[p][:C2] + _dot(a_ak[p], v_s[p]) for p in P]
        steps = int(math.log2(C))
        for j in range(steps):
            if j < steps - 1:
                x = [_dot(q[p], jnp.concatenate([q[p], u[p].astype(BF16)], axis=1)) for p in P]
                u = [u[p] + x[p][:, C2:] for p in P]
                q = [x[p][:, :C2].astype(BF16) for p in P]
            else:
                u = [u[p] + _dot(q[p], u[p].astype(BF16)) for p in P]

        vu = [jnp.concatenate([v_s[p], u[p].astype(BF16)], axis=0) for p in P]
        y = jnp.concatenate([proj[p][C2:] + _dot(a_r[p], vu[p]) for p in P], axis=1)
        kbb = [jnp.concatenate([stack(pair(kb, p)), stack(pair(bb, p))], axis=0).astype(BF16) for p in P]
        for p in P:
            state_ref[p] = st[p] * pair(e_end, p) + _dot_tn(vu[p], kbb[p])

        mean = head_sum(y) * (1.0 / RWKV_HEAD_DIM)
        dlt = y - mean
        var = head_sum(dlt * dlt) * (1.0 / RWKV_HEAD_DIM)
        yn = dlt * lax.rsqrt(var + GN_EPS) * gnw_ref[...] + gnb_ref[...]
        o_ref[rows, :] = ((yn + bonus) * g).astype(o_ref.dtype)
        return carry

    lax.fori_loop(0, n_chunks, body, 0)


def _rwkv(z, z_tail, mu_r, mu_k, mu_v, mu_t, w0, w2, a0, a2, g2, k_k, k_a, r_k, gn_w, gn_b,
          batch, seq, col0, width, bt, gw):
    m = z.shape[0]
    nt = seq // bt
    ng = width // gw
    row = lambda b, g, t: b * nt + t
    cb = col0 // gw
    wb = width // gw
    zspec = lambda off: pl.BlockSpec((bt, gw), lambda b, g, t: (row(b, g, t), cb + off * wb + g))
    vec = pl.BlockSpec((1, gw), lambda b, g, t: (0, g))
    mat = lambda r: pl.BlockSpec((r, gw), lambda b, g, t: (0, g))
    kern = functools.partial(_rwkv_kernel, n_chunks=bt // RWKV_CHUNK)
    return pl.pallas_call(
        kern,
        out_shape=jax.ShapeDtypeStruct((m, width), BF16),
        grid=(batch, ng, nt),
        in_specs=[zspec(0), zspec(1), zspec(2),
                  pl.BlockSpec((bt, TAIL_WIDTH), lambda b, g, t: (row(b, g, t), 0)),
                  vec, vec, vec,
                  pl.BlockSpec((1, TAIL_WIDTH), lambda b, g, t: (0, 0)),
                  vec, mat(RANK_PAD), vec, mat(RANK_PAD), mat(GATE_RANK),
                  vec, vec, vec, vec, vec],
        out_specs=pl.BlockSpec((bt, gw), lambda b, g, t: (row(b, g, t), g)),
        scratch_shapes=[pltpu.VMEM((1, gw), F32), pltpu.VMEM((1, gw), F32), pltpu.VMEM((1, gw), F32),
                        pltpu.VMEM((1, TAIL_WIDTH), F32),
                        pltpu.VMEM((gw // LANES, LANES, LANES), F32)],
        compiler_params=_cparams("parallel", "parallel", "arbitrary"),
        name="rwkv7",
    )(z, z, z, z_tail, mu_r, mu_k, mu_v, mu_t, w0, w2, a0, a2, g2, k_k, k_a, r_k, gn_w, gn_b)


def _pad_rows(w, rows):
    return jnp.pad(w, ((0, rows - w.shape[0]), (0, 0)))


def _pad_cols(w, cols):
    return jnp.pad(w, ((0, 0), (0, cols - w.shape[1])))


def _tiles(m, seq):
    return dict(
        norm_bm=min(256, m),
        mm_bm=min(1024, m), inproj_bn=512, outproj_bn=512, down_bm=min(512, m), down_bn=512,
        ffn_bm=min(1024, seq), ffn_bn=256, ffn_sub=2,
        lru_bt=min(256, seq), lru_bw=512,
        rwkv_bt=min(256, seq), rwkv_gw=2048,
    )


def kernel(x, g_mix, w_in, conv_lru_w, conv_lru_b, lru_w_a, lru_b_a, lru_w_i, lru_b_i, lru_lambda, rwkv_mu, rwkv_w0, rwkv_w2, rwkv_a0, rwkv_a2, rwkv_g2, rwkv_k_k, rwkv_k_a, rwkv_r_k, rwkv_gn_w, rwkv_gn_b, w_out, g_ffn, w_ffn_gate, ffn_conv_w, ffn_conv_b, w_ffn_up, w_ffn_down, g_final):
    batch, seq, d_model = x.shape
    depth = w_in.shape[0]
    lru_w = lru_lambda.shape[1]
    rw = rwkv_w0.shape[1]
    m = batch * seq
    t = _tiles(m, seq)
    xf = x.reshape(m, d_model)
    c_rkv = 2 * lru_w
    c_lr = c_rkv + 3 * rw

    def row(vv):
        return vv.reshape(1, -1)

    for l in range(depth):
        wi = w_in[l]
        w_tail = jnp.concatenate(
            [_pad_cols(wi[:, c_lr:c_lr + DECAY_RANK], RANK_PAD),
             _pad_cols(wi[:, c_lr + DECAY_RANK:c_lr + DECAY_RANK + ICLR_RANK], RANK_PAD),
             wi[:, c_lr + DECAY_RANK + ICLR_RANK:]], axis=1).astype(BF16)
        mu = rwkv_mu[l]
        mu_t = jnp.concatenate(
            [_pad_cols(row(mu[c_lr - c_rkv:c_lr - c_rkv + DECAY_RANK]), RANK_PAD),
             _pad_cols(row(mu[c_lr - c_rkv + DECAY_RANK:c_lr - c_rkv + DECAY_RANK + ICLR_RANK]), RANK_PAD),
             row(mu[c_lr - c_rkv + DECAY_RANK + ICLR_RANK:])], axis=1)

        h = _rmsnorm(xf, g_mix[l], BF16, t["norm_bm"])
        z = _matmul_wcast(h, wi, c_lr, F32, t["mm_bm"], t["inproj_bn"], "inproj")
        z_tail = _matmul(h, w_tail, F32, t["mm_bm"], TAIL_WIDTH, "inproj_tail")
        y_lru = _lru(z, conv_lru_w[l], conv_lru_b[l], lru_w_a[l], lru_b_a[l], lru_w_i[l], lru_b_i[l],
                     lru_lambda[l], batch, seq, lru_w, t["lru_bt"], t["lru_bw"])
        y_rwkv = _rwkv(z, z_tail, row(mu[:rw]), row(mu[rw:2 * rw]), row(mu[2 * rw:3 * rw]), mu_t,
                       row(rwkv_w0[l]), _pad_rows(rwkv_w2[l], RANK_PAD).astype(BF16),
                       row(rwkv_a0[l]), _pad_rows(rwkv_a2[l], RANK_PAD).astype(BF16),
                       rwkv_g2[l].astype(BF16), row(rwkv_k_k[l]), row(rwkv_k_a[l]), row(rwkv_r_k[l]),
                       row(rwkv_gn_w[l]), row(rwkv_gn_b[l]),
                       batch, seq, c_rkv, rw, t["rwkv_bt"], t["rwkv_gw"])
        xf = _outproj(y_lru, y_rwkv, w_out[l].astype(BF16), xf, t["mm_bm"], t["outproj_bn"])

        h = _rmsnorm(xf, g_ffn[l], BF16, t["norm_bm"])
        act = _ffn_up(h, w_ffn_gate[l], w_ffn_up[l], ffn_conv_w[l],
                      ffn_conv_b[l], seq, t["ffn_bm"], t["ffn_bn"], t["ffn_sub"])
        xf = _matmul_residual(act, w_ffn_down[l].astype(BF16), xf, t["down_bm"], t["down_bn"], "ffn_down")

    return _rmsnorm(xf, g_final, F32, t["norm_bm"]).reshape(batch, seq, d_model)
```

```python
import functools
import math

import jax
import jax.numpy as jnp
from jax import lax
from jax.experimental import pallas as pl
from jax.experimental.pallas import tpu as pltpu

F32 = jnp.float32
BF16 = jnp.bfloat16

V7X_VMEM_LIMIT_BYTES = 56 * 1024 * 1024
LANES = 128
SUBLANES = 8

LRU_HEAD_DIM = 128
LRU_CONV_WIDTH = 4
LRU_C = 8.0
RWKV_HEAD_DIM = 64
DECAY_RANK = 96
ICLR_RANK = 96
GATE_RANK = 256
FFN_CONV_WIDTH = 3
NORM_EPS = 1e-6
GN_EPS = 64e-5
L2_EPS = 1e-12
RANK_PAD = LANES
TAIL_WIDTH = 2 * RANK_PAD + GATE_RANK
RWKV_CHUNK = 64


def _cparams(*sem):
    return pltpu.CompilerParams(dimension_semantics=sem, vmem_limit_bytes=V7X_VMEM_LIMIT_BYTES)


def _dot(a, b):
    return jnp.dot(a, b, preferred_element_type=F32)


def _dot_nt(a, b):
    return lax.dot_general(a, b, (((1,), (1,)), ((), ())), preferred_element_type=F32)


def _dot_tn(a, b):
    return lax.dot_general(a, b, (((0,), (0,)), ((), ())), preferred_element_type=F32)


def _sigmoid(x):
    return 0.5 * jnp.tanh(0.5 * x) + 0.5


def _shift_rows(x, prev8, s):
    xe = jnp.concatenate([prev8, x], axis=0)
    return pltpu.roll(xe, s, 0)[SUBLANES:, :]


def _rmsnorm_kernel(x_ref, g_ref, o_ref):
    x = x_ref[...]
    ms = jnp.mean(x * x, axis=-1, keepdims=True)
    o_ref[...] = (x * lax.rsqrt(ms + NORM_EPS) * g_ref[...]).astype(o_ref.dtype)


def _rmsnorm(x2d, g, out_dtype, bm):
    m, d = x2d.shape
    return pl.pallas_call(
        _rmsnorm_kernel,
        out_shape=jax.ShapeDtypeStruct((m, d), out_dtype),
        grid=(m // bm,),
        in_specs=[pl.BlockSpec((bm, d), lambda i: (i, 0)),
                  pl.BlockSpec((1, d), lambda i: (0, 0))],
        out_specs=pl.BlockSpec((bm, d), lambda i: (i, 0)),
        compiler_params=_cparams("parallel"),
        name="rmsnorm",
    )(x2d, g.reshape(1, d))


def _mm_kernel(a_ref, b_ref, o_ref):
    o_ref[...] = _dot(a_ref[...], b_ref[...]).astype(o_ref.dtype)


def _matmul(a, b, out_dtype, bm, bn, name):
    m, k = a.shape
    n = b.shape[1]
    return pl.pallas_call(
        _mm_kernel,
        out_shape=jax.ShapeDtypeStruct((m, n), out_dtype),
        grid=(m // bm, n // bn),
        in_specs=[pl.BlockSpec((bm, k), lambda i, j: (i, 0)),
                  pl.BlockSpec((k, bn), lambda i, j: (0, j))],
        out_specs=pl.BlockSpec((bm, bn), lambda i, j: (i, j)),
        compiler_params=_cparams("parallel", "parallel"),
        name=name,
    )(a, b)


def _mm_wcast_kernel(a_ref, w_ref, o_ref, wb_ref):
    @pl.when(pl.program_id(1) == 0)
    def _():
        wb_ref[...] = w_ref[...].astype(BF16)

    o_ref[...] = _dot(a_ref[...], wb_ref[...]).astype(o_ref.dtype)


def _matmul_wcast(a, w_f32, n, out_dtype, bm, bn, name):
    m, k = a.shape
    return pl.pallas_call(
        _mm_wcast_kernel,
        out_shape=jax.ShapeDtypeStruct((m, n), out_dtype),
        grid=(n // bn, m // bm),
        in_specs=[pl.BlockSpec((bm, k), lambda j, i: (i, 0)),
                  pl.BlockSpec((k, bn), lambda j, i: (0, j))],
        out_specs=pl.BlockSpec((bm, bn), lambda j, i: (i, j)),
        scratch_shapes=[pltpu.VMEM((k, bn), BF16)],
        compiler_params=_cparams("parallel", "arbitrary"),
        name=name,
    )(a, w_f32)


def _mm_wcast_t_kernel(a_ref, wt_ref, o_ref, wb_ref, *, tb):
    @pl.when(pl.program_id(1) == 0)
    def _():
        for c in range(wt_ref.shape[1] // tb):
            wb_ref[c * tb:(c + 1) * tb, :] = wt_ref[:, c * tb:(c + 1) * tb].T.astype(BF16)

    o_ref[...] = _dot(a_ref[...], wb_ref[...]).astype(o_ref.dtype)


def _matmul_wcast_t(a, wt_f32, n, out_dtype, bm, bn, name):
    m, k = a.shape
    kern = functools.partial(_mm_wcast_t_kernel, tb=bn)
    return pl.pallas_call(
        kern,
        out_shape=jax.ShapeDtypeStruct((m, n), out_dtype),
        grid=(n // bn, m // bm),
        in_specs=[pl.BlockSpec((bm, k), lambda j, i: (i, 0)),
                  pl.BlockSpec((bn, k), lambda j, i: (j, 0))],
        out_specs=pl.BlockSpec((bm, bn), lambda j, i: (i, j)),
        scratch_shapes=[pltpu.VMEM((k, bn), BF16)],
        compiler_params=_cparams("parallel", "arbitrary"),
        name=name,
    )(a, wt_f32)


def _mm_res_kernel(a_ref, b_ref, r_ref, o_ref):
    o_ref[...] = r_ref[...] + _dot(a_ref[...], b_ref[...])


def _matmul_residual(a, b, res, bm, bn, name):
    m, k = a.shape
    n = b.shape[1]
    return pl.pallas_call(
        _mm_res_kernel,
        out_shape=jax.ShapeDtypeStruct((m, n), F32),
        grid=(m // bm, n // bn),
        in_specs=[pl.BlockSpec((bm, k), lambda i, j: (i, 0)),
                  pl.BlockSpec((k, bn), lambda i, j: (0, j)),
                  pl.BlockSpec((bm, bn), lambda i, j: (i, j))],
        out_specs=pl.BlockSpec((bm, bn), lambda i, j: (i, j)),
        compiler_params=_cparams("parallel", "parallel"),
        name=name,
    )(a, b, res)


def _outproj_kernel(a1_ref, a2_ref, w1_ref, w2_ref, r_ref, o_ref):
    acc = _dot(a1_ref[...], w1_ref[...]) + _dot(a2_ref[...], w2_ref[...])
    o_ref[...] = r_ref[...] + acc


def _outproj(y1, y2, w, res, bm, bn):
    m, k1 = y1.shape
    k2 = y2.shape[1]
    assert k1 == k2
    n = w.shape[1]
    return pl.pallas_call(
        _outproj_kernel,
        out_shape=jax.ShapeDtypeStruct((m, n), F32),
        grid=(m // bm, n // bn),
        in_specs=[pl.BlockSpec((bm, k1), lambda i, j: (i, 0)),
                  pl.BlockSpec((bm, k2), lambda i, j: (i, 0)),
                  pl.BlockSpec((k1, bn), lambda i, j: (0, j)),
                  pl.BlockSpec((k2, bn), lambda i, j: (1, j)),
                  pl.BlockSpec((bm, bn), lambda i, j: (i, j))],
        out_specs=pl.BlockSpec((bm, bn), lambda i, j: (i, j)),
        compiler_params=_cparams("parallel", "parallel"),
        name="outproj",
    )(y1, y2, w, w, res)


def _ffn_up_kernel(h_ref, wg0_ref, wu0_ref, wgc_ref, wuc_ref, cw_ref, cb_ref, o_ref,
                   prev_ref, wgb_ref, wub_ref, *, tiles_per_seq, n_sub):
    j = pl.program_id(0)
    i = pl.program_id(1)
    slot = j % 2
    ck = wgc_ref.shape[0]

    @pl.when((j == 0) & (i == 0))
    def _():
        wgb_ref[0] = wg0_ref[...]
        wub_ref[0] = wu0_ref[...]

    @pl.when(i % tiles_per_seq == 0)
    def _():
        prev_ref[...] = jnp.zeros_like(prev_ref)

    cw = cw_ref[...]
    prev8 = prev_ref[...]
    sub = o_ref.shape[0] // n_sub
    for s in range(n_sub):
        rows = slice(s * sub, (s + 1) * sub)
        h = h_ref[rows, :]
        gate = _dot(h, wgb_ref[slot])
        up = _dot(h, wub_ref[slot])
        conv = (cw[2:3, :] * gate + cw[1:2, :] * _shift_rows(gate, prev8, 1)
                + cw[0:1, :] * _shift_rows(gate, prev8, 2) + cb_ref[...])
        o_ref[rows, :] = (conv * _sigmoid(conv) * up).astype(o_ref.dtype)
        prev8 = gate[-SUBLANES:, :]
    prev_ref[...] = prev8

    chunk = pl.ds(pl.multiple_of(i * ck, ck), ck)
    wgb_ref[1 - slot, chunk, :] = wgc_ref[...].astype(BF16)
    wub_ref[1 - slot, chunk, :] = wuc_ref[...].astype(BF16)


def _ffn_up(h, wg, wu, conv_w, conv_b, seq, bm, bn, n_sub):
    m, k = h.shape
    n = wg.shape[1]
    n_i = m // bm
    n_j = n // bn
    ck = k // n_i
    nxt = lambda j: jnp.minimum(j + 1, n_j - 1)
    kern = functools.partial(_ffn_up_kernel, tiles_per_seq=seq // bm, n_sub=n_sub)
    return pl.pallas_call(
        kern,
        out_shape=jax.ShapeDtypeStruct((m, n), BF16),
        grid=(n_j, n_i),
        in_specs=[pl.BlockSpec((bm, k), lambda j, i: (i, 0)),
                  pl.BlockSpec((k, bn), lambda j, i: (0, 0)),
                  pl.BlockSpec((k, bn), lambda j, i: (0, 0)),
                  pl.BlockSpec((ck, bn), lambda j, i: (i, nxt(j))),
                  pl.BlockSpec((ck, bn), lambda j, i: (i, nxt(j))),
                  pl.BlockSpec((FFN_CONV_WIDTH, bn), lambda j, i: (0, j)),
                  pl.BlockSpec((1, bn), lambda j, i: (0, j))],
        out_specs=pl.BlockSpec((bm, bn), lambda j, i: (i, j)),
        scratch_shapes=[pltpu.VMEM((SUBLANES, bn), F32), pltpu.VMEM((2, k, bn), BF16),
                        pltpu.VMEM((2, k, bn), BF16)],
        compiler_params=_cparams("arbitrary", "arbitrary"),
        name="ffn_up",
    )(h, wg[:, :bn].astype(BF16), wu[:, :bn].astype(BF16), wg, wu, conv_w, conv_b.reshape(1, n))


def _gelu_tanh(x):
    c = math.sqrt(2.0 / math.pi)
    return 0.5 * x * (1.0 + jnp.tanh(c * (x + 0.044715 * (x * x * x))))


def _softplus(y):
    return jnp.maximum(y, 0.0) + jnp.log1p(jnp.exp(-jnp.abs(y)))


def _lru_kernel(x_ref, gt_ref, cw_ref, cb_ref, wa_ref, ba_ref, wi_ref, bi_ref, lam_ref,
                o_ref, prevx_ref, h_ref):
    t = pl.program_id(2)

    @pl.when(t == 0)
    def _():
        prevx_ref[...] = jnp.zeros_like(prevx_ref)
        h_ref[...] = jnp.zeros_like(h_ref)

    x = x_ref[...]
    rows, width = x.shape
    prev8 = prevx_ref[...]
    prevx_ref[...] = x[-SUBLANES:, :]
    cw = cw_ref[...]
    xc = cw[3:4, :] * x + cb_ref[...]
    for s in range(1, LRU_CONV_WIDTH):
        xc = xc + cw[LRU_CONV_WIDTH - 1 - s:LRU_CONV_WIDTH - s, :] * _shift_rows(x, prev8, s)

    xcb = xc.astype(BF16)
    heads = width // LRU_HEAD_DIM
    pre_a = []
    pre_i = []
    for hd in range(heads):
        xh = xcb[:, hd * LRU_HEAD_DIM:(hd + 1) * LRU_HEAD_DIM]
        pre_a.append(_dot(xh, wa_ref[hd]))
        pre_i.append(_dot(xh, wi_ref[hd]))
    r = _sigmoid(jnp.concatenate(pre_a, axis=1) + ba_ref[...])
    gi = _sigmoid(jnp.concatenate(pre_i, axis=1) + bi_ref[...])
    log_a = (-LRU_C) * r * _softplus(-lam_ref[...])
    a = jnp.exp(log_a)
    th = jnp.tanh(log_a)
    num = -2.0 * th
    mult = jnp.where(num > 0.0, num * lax.rsqrt(num * (1.0 - th)), 0.0)
    u = mult * (gi * xc)

    n_groups = rows // SUBLANES
    a = a.reshape(n_groups, SUBLANES, width)
    u = u.reshape(n_groups, SUBLANES, width)
    sub = lax.broadcasted_iota(jnp.int32, (n_groups, SUBLANES, width), 1)
    d = 1
    while d < SUBLANES:
        keep = sub >= d
        a_sh = jnp.where(keep, pltpu.roll(a, d, 1), 1.0)
        u_sh = jnp.where(keep, pltpu.roll(u, d, 1), 0.0)
        u = u + a * u_sh
        a = a * a_sh
        d *= 2
    carry = h_ref[...]
    groups = []
    for gi in range(n_groups):
        hg = u[gi] + a[gi] * carry
        groups.append(hg)
        carry = hg[SUBLANES - 1:SUBLANES, :]
    h = jnp.concatenate(groups, axis=0)
    h_ref[...] = carry
    o_ref[...] = (h * _gelu_tanh(gt_ref[...])).astype(o_ref.dtype)


def _lru(z, conv_w, conv_b, w_a, b_a, w_i, b_i, lam, batch, seq, lru_width, bt, bw):
    m = z.shape[0]
    nt = seq // bt
    ncb = lru_width // bw
    hpb = bw // LRU_HEAD_DIM
    row = lambda b, c, t: b * nt + t
    vec = pl.BlockSpec((1, bw), lambda b, c, t: (0, c))
    return pl.pallas_call(
        _lru_kernel,
        out_shape=jax.ShapeDtypeStruct((m, lru_width), BF16),
        grid=(batch, ncb, nt),
        in_specs=[pl.BlockSpec((bt, bw), lambda b, c, t: (row(b, c, t), c)),
                  pl.BlockSpec((bt, bw), lambda b, c, t: (row(b, c, t), ncb + c)),
                  pl.BlockSpec((LRU_CONV_WIDTH, bw), lambda b, c, t: (0, c)),
                  vec,
                  pl.BlockSpec((hpb, LRU_HEAD_DIM, LRU_HEAD_DIM), lambda b, c, t: (c, 0, 0)),
                  vec,
                  pl.BlockSpec((hpb, LRU_HEAD_DIM, LRU_HEAD_DIM), lambda b, c, t: (c, 0, 0)),
                  vec, vec],
        out_specs=pl.BlockSpec((bt, bw), lambda b, c, t: (row(b, c, t), c)),
        scratch_shapes=[pltpu.VMEM((SUBLANES, bw), F32), pltpu.VMEM((1, bw), F32)],
        compiler_params=_cparams("parallel", "parallel", "arbitrary"),
        name="rg_lru",
    )(z, z, conv_w, conv_b.reshape(1, -1), w_a.astype(BF16), b_a.reshape(1, -1),
      w_i.astype(BF16), b_i.reshape(1, -1), lam.reshape(1, -1))


def _split_bf16(x):
    hi = x.astype(BF16)
    lo = (x - hi.astype(F32)).astype(BF16)
    return hi, lo


def _rwkv_kernel(zr_ref, zk_ref, zv_ref, zt_ref, mur_ref, muk_ref, muv_ref, mut_ref,
                 w0_ref, w2_ref, a0_ref, a2_ref, g2_ref, kk_ref, ka_ref, rk_ref, gnw_ref, gnb_ref,
                 o_ref, pr_ref, pk_ref, pv_ref, pt_ref, state_ref, *, n_chunks):
    C = RWKV_CHUNK
    C2 = 2 * C
    gw = o_ref.shape[1]
    pairs = gw // LANES
    seg_w = 2 * LANES

    @pl.when(pl.program_id(2) == 0)
    def _():
        pr_ref[...] = jnp.zeros_like(pr_ref)
        pk_ref[...] = jnp.zeros_like(pk_ref)
        pv_ref[...] = jnp.zeros_like(pv_ref)
        pt_ref[...] = jnp.zeros_like(pt_ref)
        state_ref[...] = jnp.zeros_like(state_ref)

    ri = lax.broadcasted_iota(jnp.int32, (C2, C2), 0)
    ci = lax.broadcasted_iota(jnp.int32, (C2, C2), 1)
    strict = ((ri >= C) == (ci >= C)) & (ri > ci)
    rt_i = lax.broadcasted_iota(jnp.int32, (C, C2), 0)
    ct_i = lax.broadcasted_iota(jnp.int32, (C, C2), 1)
    incl = jnp.where(ct_i >= C, ct_i - C, ct_i) <= rt_i
    tri3 = (lax.broadcasted_iota(jnp.int32, (C, 3 * C), 0)
            >= lax.broadcasted_iota(jnp.int32, (C, 3 * C), 1) % C).astype(BF16)
    head0 = lax.broadcasted_iota(jnp.int32, (C, LANES), 1) < RWKV_HEAD_DIM
    seg = ((lax.broadcasted_iota(jnp.int32, (seg_w, seg_w), 0) // RWKV_HEAD_DIM)
           == (lax.broadcasted_iota(jnp.int32, (seg_w, seg_w), 1) // RWKV_HEAD_DIM)).astype(BF16)
    first_row = lax.broadcasted_iota(jnp.int32, (C, 1), 0) == 0
    P = range(pairs)

    def pair(x, p):
        return x[:, p * LANES:(p + 1) * LANES]

    def stack(x):
        return jnp.concatenate([jnp.where(head0, x, 0.0), jnp.where(head0, 0.0, x)], axis=0)

    def head_sum(x):
        xb = x.astype(BF16)
        return jnp.concatenate(
            [_dot(xb[:, q * seg_w:(q + 1) * seg_w], seg) for q in range(gw // seg_w)], axis=1)

    def mix(z_ref, p_ref, mu_ref, rows):
        z = z_ref[rows, :]
        rolled = pltpu.roll(z, 1, 0)
        sh = jnp.where(first_row, p_ref[...], rolled)
        p_ref[...] = rolled[0:1, :]
        return z + (sh - z) * mu_ref[...]

    def body(cidx, carry):
        rows = pl.ds(pl.multiple_of(cidx * C, C), C)
        r = mix(zr_ref, pr_ref, mur_ref, rows)
        k = mix(zk_ref, pk_ref, muk_ref, rows)
        v = mix(zv_ref, pv_ref, muv_ref, rows)
        tl = mix(zt_ref, pt_ref, mut_ref, rows)
        zw = jnp.tanh(tl[:, :RANK_PAD]).astype(BF16)
        za = tl[:, RANK_PAD:2 * RANK_PAD].astype(BF16)
        zg = _sigmoid(tl[:, 2 * RANK_PAD:]).astype(BF16)
        logw = (-math.exp(-0.5)) * _sigmoid(w0_ref[...] + _dot(zw, w2_ref[...]))
        a = _sigmoid(a0_ref[...] + _dot(za, a2_ref[...]))
        g = _dot(zg, g2_ref[...])

        kkp = k * kk_ref[...]
        kk = kkp * jnp.minimum(lax.rsqrt(head_sum(kkp * kkp)), 1.0 / L2_EPS)
        kp = k * (1.0 + (a - 1.0) * ka_ref[...])
        b = kk * a
        bonus = head_sum(r * kp * rk_ref[...]) * v

        l1 = logw.astype(BF16)
        rem = logw - l1.astype(F32)
        l2 = rem.astype(BF16)
        l3 = (rem - l2.astype(F32)).astype(BF16)
        lw = _dot(tri3, jnp.concatenate([l1, l2, l3], axis=0))
        lw_end = lw[C - 1:C, :]
        e_neg = jnp.exp(-lw)
        e_rem = jnp.exp(lw_end - lw)
        e_end = jnp.exp(lw_end)
        at = -kk * jnp.exp(lw - logw)
        rt = (r * jnp.exp(lw)).astype(BF16)
        kh = kp * e_neg
        bh = b * e_neg
        kb = kp * e_rem
        bb = b * e_rem

        at_s = [stack(pair(at, p)).astype(BF16) for p in P]
        v_s = [stack(pair(v, p)).astype(BF16) for p in P]
        lhs = [jnp.concatenate([at_s[p], pair(rt, p)], axis=0) for p in P]
        rhs = [jnp.concatenate([stack(pair(kh, p)), stack(pair(bh, p))], axis=0).astype(BF16) for p in P]
        gram = [_dot_nt(lhs[p], rhs[p]) for p in P]
        st = [state_ref[p] for p in P]
        proj = [_dot_nt(lhs[p], st[p].astype(BF16)) for p in P]
        a_ak = [jnp.where(strict, gram[p][:C2, :C2], 0.0).astype(BF16) for p in P]
        q = [jnp.where(strict, gram[p][:C2, C2:], 0.0).astype(BF16) for p in P]
        a_r = [jnp.concatenate([jnp.where(incl, gram[p][C2:, :C2], 0.0),
                                jnp.where(incl, gram[p][C2:, C2:], 0.0)], axis=1).astype(BF16) for p in P]

        u = [proj[p][:C2] + _dot(a_ak[p], v_s[p]) for p in P]
        steps = int(math.log2(C))
        for j in range(steps):
            if j < steps - 1:
                x = [_dot(q[p], jnp.concatenate([q[p], u[p].astype(BF16)], axis=1)) for p in P]
                u = [u[p] + x[p][:, C2:] for p in P]
                q = [x[p][:, :C2].astype(BF16) for p in P]
            else:
                u = [u[p] + _dot(q[p], u[p].astype(BF16)) for p in P]

        vu = [jnp.concatenate([v_s[p], u[p].astype(BF16)], axis=0) for p in P]
        y = jnp.concatenate([proj[p][C2:] + _dot(a_r[p], vu[p]) for p in P], axis=1)
        kbb = [jnp.concatenate([stack(pair(kb, p)), stack(pair(bb, p))], axis=0).astype(BF16) for p in P]
        for p in P:
            state_ref[p] = st[p] * pair(e_end, p) + _dot_tn(vu[p], kbb[p])

        mean = head_sum(y) * (1.0 / RWKV_HEAD_DIM)
        dlt = y - mean
        var = head_sum(dlt * dlt) * (1.0 / RWKV_HEAD_DIM)
        yn = dlt * lax.rsqrt(var + GN_EPS) * gnw_ref[...] + gnb_ref[...]
        o_ref[rows, :] = ((yn + bonus) * g).astype(o_ref.dtype)
        return carry

    lax.fori_loop(0, n_chunks, body, 0)


def _rwkv(z, z_tail, mu_r, mu_k, mu_v, mu_t, w0, w2, a0, a2, g2, k_k, k_a, r_k, gn_w, gn_b,
          batch, seq, col0, width, bt, gw):
    m = z.shape[0]
    nt = seq // bt
    ng = width // gw
    row = lambda b, g, t: b * nt + t
    cb = col0 // gw
    wb = width // gw
    zspec = lambda off: pl.BlockSpec((bt, gw), lambda b, g, t: (row(b, g, t), cb + off * wb + g))
    vec = pl.BlockSpec((1, gw), lambda b, g, t: (0, g))
    mat = lambda r: pl.BlockSpec((r, gw), lambda b, g, t: (0, g))
    kern = functools.partial(_rwkv_kernel, n_chunks=bt // RWKV_CHUNK)
    return pl.pallas_call(
        kern,
        out_shape=jax.ShapeDtypeStruct((m, width), BF16),
        grid=(batch, ng, nt),
        in_specs=[zspec(0), zspec(1), zspec(2),
                  pl.BlockSpec((bt, TAIL_WIDTH), lambda b, g, t: (row(b, g, t), 0)),
                  vec, vec, vec,
                  pl.BlockSpec((1, TAIL_WIDTH), lambda b, g, t: (0, 0)),
                  vec, mat(RANK_PAD), vec, mat(RANK_PAD), mat(GATE_RANK),
                  vec, vec, vec, vec, vec],
        out_specs=pl.BlockSpec((bt, gw), lambda b, g, t: (row(b, g, t), g)),
        scratch_shapes=[pltpu.VMEM((1, gw), F32), pltpu.VMEM((1, gw), F32), pltpu.VMEM((1, gw), F32),
                        pltpu.VMEM((1, TAIL_WIDTH), F32),
                        pltpu.VMEM((gw // LANES, LANES, LANES), F32)],
        compiler_params=_cparams("parallel", "parallel", "arbitrary"),
        name="rwkv7",
    )(z, z, z, z_tail, mu_r, mu_k, mu_v, mu_t, w0, w2, a0, a2, g2, k_k, k_a, r_k, gn_w, gn_b)


def _pad_rows(w, rows):
    return jnp.pad(w, ((0, rows - w.shape[0]), (0, 0)))


def _pad_cols(w, cols):
    return jnp.pad(w, ((0, 0), (0, cols - w.shape[1])))


def _tiles(m, seq):
    return dict(
        norm_bm=min(256, m),
        mm_bm=min(1024, m), inproj_bn=512, outproj_bn=1024, down_bm=min(512, m), down_bn=512,
        ffn_bm=min(2048, seq), ffn_bn=256, ffn_sub=2,
        lru_bt=min(256, seq), lru_bw=512,
        rwkv_bt=min(256, seq), rwkv_gw=2048,
    )


def kernel(x, g_mix, w_in, conv_lru_w, conv_lru_b, lru_w_a, lru_b_a, lru_w_i, lru_b_i, lru_lambda, rwkv_mu, rwkv_w0, rwkv_w2, rwkv_a0, rwkv_a2, rwkv_g2, rwkv_k_k, rwkv_k_a, rwkv_r_k, rwkv_gn_w, rwkv_gn_b, w_out, g_ffn, w_ffn_gate, ffn_conv_w, ffn_conv_b, w_ffn_up, w_ffn_down, g_final):
    batch, seq, d_model = x.shape
    depth = w_in.shape[0]
    lru_w = lru_lambda.shape[1]
    rw = rwkv_w0.shape[1]
    m = batch * seq
    t = _tiles(m, seq)
    xf = x.reshape(m, d_model)
    c_rkv = 2 * lru_w
    c_lr = c_rkv + 3 * rw

    def row(vv):
        return vv.reshape(1, -1)

    for l in range(depth):
        wi_t = jnp.swapaxes(w_in[l], 0, 1)
        wt = wi_t[c_lr:, :]
        w_tail_t = jnp.concatenate(
            [_pad_rows(wt[:DECAY_RANK], RANK_PAD),
             _pad_rows(wt[DECAY_RANK:DECAY_RANK + ICLR_RANK], RANK_PAD),
             wt[DECAY_RANK + ICLR_RANK:]], axis=0)
        mu = rwkv_mu[l]
        mu_t = jnp.concatenate(
            [_pad_cols(row(mu[c_lr - c_rkv:c_lr - c_rkv + DECAY_RANK]), RANK_PAD),
             _pad_cols(row(mu[c_lr - c_rkv + DECAY_RANK:c_lr - c_rkv + DECAY_RANK + ICLR_RANK]), RANK_PAD),
             row(mu[c_lr - c_rkv + DECAY_RANK + ICLR_RANK:])], axis=1)

        h = _rmsnorm(xf, g_mix[l], BF16, t["norm_bm"])
        z = _matmul_wcast_t(h, wi_t, c_lr, F32, t["mm_bm"], t["inproj_bn"], "inproj")
        z_tail = _matmul_wcast_t(h, w_tail_t, TAIL_WIDTH, F32, t["mm_bm"], TAIL_WIDTH, "inproj_tail")
        y_lru = _lru(z, conv_lru_w[l], conv_lru_b[l], lru_w_a[l], lru_b_a[l], lru_w_i[l], lru_b_i[l],
                     lru_lambda[l], batch, seq, lru_w, t["lru_bt"], t["lru_bw"])
        y_rwkv = _rwkv(z, z_tail, row(mu[:rw]), row(mu[rw:2 * rw]), row(mu[2 * rw:3 * rw]), mu_t,
                       row(rwkv_w0[l]), _pad_rows(rwkv_w2[l], RANK_PAD).astype(BF16),
                       row(rwkv_a0[l]), _pad_rows(rwkv_a2[l], RANK_PAD).astype(BF16),
                       rwkv_g2[l].astype(BF16), row(rwkv_k_k[l]), row(rwkv_k_a[l]), row(rwkv_r_k[l]),
                       row(rwkv_gn_w[l]), row(rwkv_gn_b[l]),
                       batch, seq, c_rkv, rw, t["rwkv_bt"], t["rwkv_gw"])
        xf = _outproj(y_lru, y_rwkv, w_out[l].astype(BF16), xf, t["mm_bm"], t["outproj_bn"])

        h = _rmsnorm(xf, g_ffn[l], BF16, t["norm_bm"])
        act = _ffn_up(h, w_ffn_gate[l], w_ffn_up[l], ffn_conv_w[l],
                      ffn_conv_b[l], seq, t["ffn_bm"], t["ffn_bn"], t["ffn_sub"])
        xf = _matmul_residual(act, w_ffn_down[l].astype(BF16), xf, t["down_bm"], t["down_bn"], "ffn_down")

    return _rmsnorm(xf, g_final, F32, t["norm_bm"]).reshape(batch, seq, d_model)
```

```python
import functools
import math

import jax
import jax.numpy as jnp
from jax import lax
from jax.experimental import pallas as pl
from jax.experimental.pallas import tpu as pltpu

F32 = jnp.float32
BF16 = jnp.bfloat16

V7X_VMEM_LIMIT_BYTES = 56 * 1024 * 1024
LANES = 128
SUBLANES = 8

LRU_HEAD_DIM = 128
LRU_CONV_WIDTH = 4
LRU_C = 8.0
RWKV_HEAD_DIM = 64
DECAY_RANK = 96
ICLR_RANK = 96
GATE_RANK = 256
FFN_CONV_WIDTH = 3
NORM_EPS = 1e-6
GN_EPS = 64e-5
L2_EPS = 1e-12
RANK_PAD = LANES
TAIL_WIDTH = 2 * RANK_PAD + GATE_RANK
RWKV_CHUNK = 64


def _cparams(*sem):
    return pltpu.CompilerParams(dimension_semantics=sem, vmem_limit_bytes=V7X_VMEM_LIMIT_BYTES)


def _dot(a, b):
    return jnp.dot(a, b, preferred_element_type=F32)


def _dot_nt(a, b):
    return lax.dot_general(a, b, (((1,), (1,)), ((), ())), preferred_element_type=F32)


def _dot_tn(a, b):
    return lax.dot_general(a, b, (((0,), (0,)), ((), ())), preferred_element_type=F32)


def _sigmoid(x):
    return 0.5 * jnp.tanh(0.5 * x) + 0.5


def _shift_rows(x, prev8, s):
    xe = jnp.concatenate([prev8, x], axis=0)
    return pltpu.roll(xe, s, 0)[SUBLANES:, :]


def _rmsnorm_kernel(x_ref, g_ref, o_ref):
    x = x_ref[...]
    ms = jnp.mean(x * x, axis=-1, keepdims=True)
    o_ref[...] = (x * lax.rsqrt(ms + NORM_EPS) * g_ref[...]).astype(o_ref.dtype)


def _rmsnorm(x2d, g, out_dtype, bm):
    m, d = x2d.shape
    return pl.pallas_call(
        _rmsnorm_kernel,
        out_shape=jax.ShapeDtypeStruct((m, d), out_dtype),
        grid=(m // bm,),
        in_specs=[pl.BlockSpec((bm, d), lambda i: (i, 0)),
                  pl.BlockSpec((1, d), lambda i: (0, 0))],
        out_specs=pl.BlockSpec((bm, d), lambda i: (i, 0)),
        compiler_params=_cparams("parallel"),
        name="rmsnorm",
    )(x2d, g.reshape(1, d))


def _mm_kernel(a_ref, b_ref, o_ref):
    o_ref[...] = _dot(a_ref[...], b_ref[...]).astype(o_ref.dtype)


def _matmul(a, b, out_dtype, bm, bn, name):
    m, k = a.shape
    n = b.shape[1]
    return pl.pallas_call(
        _mm_kernel,
        out_shape=jax.ShapeDtypeStruct((m, n), out_dtype),
        grid=(m // bm, n // bn),
        in_specs=[pl.BlockSpec((bm, k), lambda i, j: (i, 0)),
                  pl.BlockSpec((k, bn), lambda i, j: (0, j))],
        out_specs=pl.BlockSpec((bm, bn), lambda i, j: (i, j)),
        compiler_params=_cparams("parallel", "parallel"),
        name=name,
    )(a, b)


def _mm_wt_ahead_kernel(a_ref, wc_ref, o_ref, wb_ref):
    j = pl.program_id(0)
    i = pl.program_id(1)
    ck = wc_ref.shape[1]
    piece = pl.ds(pl.multiple_of(i * ck, ck), ck)

    @pl.when(j == 0)
    def _():
        wb_ref[0, piece, :] = wc_ref[...].T.astype(BF16)

    @pl.when(j > 0)
    def _():
        slot = (j - 1) % 2
        o_ref[...] = _dot(a_ref[...], wb_ref[slot]).astype(o_ref.dtype)
        wb_ref[1 - slot, piece, :] = wc_ref[...].T.astype(BF16)


def _matmul_wt_ahead(a, wt_f32, n, out_dtype, bm, bn, name):
    m, k = a.shape
    n_i = m // bm
    n_j = n // bn
    ck = k // n_i
    row = lambda j, i: jnp.where(j > 0, i, 0)
    return pl.pallas_call(
        _mm_wt_ahead_kernel,
        out_shape=jax.ShapeDtypeStruct((m, n), out_dtype),
        grid=(n_j + 1, n_i),
        in_specs=[pl.BlockSpec((bm, k), lambda j, i: (row(j, i), 0)),
                  pl.BlockSpec((bn, ck), lambda j, i: (jnp.minimum(j, n_j - 1), i))],
        out_specs=pl.BlockSpec((bm, bn), lambda j, i: (row(j, i), jnp.maximum(j - 1, 0))),
        scratch_shapes=[pltpu.VMEM((2, k, bn), BF16)],
        compiler_params=_cparams("arbitrary", "arbitrary"),
        name=name,
    )(a, wt_f32)


def _mm_wcast_t_kernel(a_ref, wt_ref, o_ref, wb_ref, *, tb):
    @pl.when(pl.program_id(1) == 0)
    def _():
        for c in range(wt_ref.shape[1] // tb):
            wb_ref[c * tb:(c + 1) * tb, :] = wt_ref[:, c * tb:(c + 1) * tb].T.astype(BF16)

    o_ref[...] = _dot(a_ref[...], wb_ref[...]).astype(o_ref.dtype)


def _matmul_wcast_t(a, wt_f32, n, out_dtype, bm, bn, name):
    m, k = a.shape
    kern = functools.partial(_mm_wcast_t_kernel, tb=bn)
    return pl.pallas_call(
        kern,
        out_shape=jax.ShapeDtypeStruct((m, n), out_dtype),
        grid=(n // bn, m // bm),
        in_specs=[pl.BlockSpec((bm, k), lambda j, i: (i, 0)),
                  pl.BlockSpec((bn, k), lambda j, i: (j, 0))],
        out_specs=pl.BlockSpec((bm, bn), lambda j, i: (i, j)),
        scratch_shapes=[pltpu.VMEM((k, bn), BF16)],
        compiler_params=_cparams("parallel", "arbitrary"),
        name=name,
    )(a, wt_f32)


def _mm_res_kernel(a_ref, b_ref, r_ref, o_ref):
    o_ref[...] = r_ref[...] + _dot(a_ref[...], b_ref[...])


def _matmul_residual(a, b, res, bm, bn, name):
    m, k = a.shape
    n = b.shape[1]
    return pl.pallas_call(
        _mm_res_kernel,
        out_shape=jax.ShapeDtypeStruct((m, n), F32),
        grid=(m // bm, n // bn),
        in_specs=[pl.BlockSpec((bm, k), lambda i, j: (i, 0)),
                  pl.BlockSpec((k, bn), lambda i, j: (0, j)),
                  pl.BlockSpec((bm, bn), lambda i, j: (i, j))],
        out_specs=pl.BlockSpec((bm, bn), lambda i, j: (i, j)),
        compiler_params=_cparams("parallel", "parallel"),
        name=name,
    )(a, b, res)


def _outproj_kernel(a1_ref, a2_ref, w1_ref, w2_ref, r_ref, o_ref):
    acc = _dot(a1_ref[...], w1_ref[...]) + _dot(a2_ref[...], w2_ref[...])
    o_ref[...] = r_ref[...] + acc


def _outproj(y1, y2, w, res, bm, bn):
    m, k1 = y1.shape
    k2 = y2.shape[1]
    assert k1 == k2
    n = w.shape[1]
    return pl.pallas_call(
        _outproj_kernel,
        out_shape=jax.ShapeDtypeStruct((m, n), F32),
        grid=(m // bm, n // bn),
        in_specs=[pl.BlockSpec((bm, k1), lambda i, j: (i, 0)),
                  pl.BlockSpec((bm, k2), lambda i, j: (i, 0)),
                  pl.BlockSpec((k1, bn), lambda i, j: (0, j)),
                  pl.BlockSpec((k2, bn), lambda i, j: (1, j)),
                  pl.BlockSpec((bm, bn), lambda i, j: (i, j))],
        out_specs=pl.BlockSpec((bm, bn), lambda i, j: (i, j)),
        compiler_params=_cparams("parallel", "parallel"),
        name="outproj",
    )(y1, y2, w, w, res)


def _ffn_up_kernel(h_ref, wg0_ref, wu0_ref, wgc_ref, wuc_ref, cw_ref, cb_ref, o_ref,
                   prev_ref, wgb_ref, wub_ref, *, tiles_per_seq, n_sub):
    j = pl.program_id(0)
    i = pl.program_id(1)
    slot = j % 2
    ck = wgc_ref.shape[0]

    @pl.when((j == 0) & (i == 0))
    def _():
        wgb_ref[0] = wg0_ref[...]
        wub_ref[0] = wu0_ref[...]

    @pl.when(i % tiles_per_seq == 0)
    def _():
        prev_ref[...] = jnp.zeros_like(prev_ref)

    cw = cw_ref[...]
    prev8 = prev_ref[...]
    sub = o_ref.shape[0] // n_sub
    for s in range(n_sub):
        rows = slice(s * sub, (s + 1) * sub)
        h = h_ref[rows, :]
        gate = _dot(h, wgb_ref[slot])
        up = _dot(h, wub_ref[slot])
        conv = (cw[2:3, :] * gate + cw[1:2, :] * _shift_rows(gate, prev8, 1)
                + cw[0:1, :] * _shift_rows(gate, prev8, 2) + cb_ref[...])
        o_ref[rows, :] = (conv * _sigmoid(conv) * up).astype(o_ref.dtype)
        prev8 = gate[-SUBLANES:, :]
    prev_ref[...] = prev8

    chunk = pl.ds(pl.multiple_of(i * ck, ck), ck)
    wgb_ref[1 - slot, chunk, :] = wgc_ref[...].astype(BF16)
    wub_ref[1 - slot, chunk, :] = wuc_ref[...].astype(BF16)


def _ffn_up(h, wg, wu, conv_w, conv_b, seq, bm, bn, n_sub):
    m, k = h.shape
    n = wg.shape[1]
    n_i = m // bm
    n_j = n // bn
    ck = k // n_i
    nxt = lambda j: jnp.minimum(j + 1, n_j - 1)
    kern = functools.partial(_ffn_up_kernel, tiles_per_seq=seq // bm, n_sub=n_sub)
    return pl.pallas_call(
        kern,
        out_shape=jax.ShapeDtypeStruct((m, n), BF16),
        grid=(n_j, n_i),
        in_specs=[pl.BlockSpec((bm, k), lambda j, i: (i, 0)),
                  pl.BlockSpec((k, bn), lambda j, i: (0, 0)),
                  pl.BlockSpec((k, bn), lambda j, i: (0, 0)),
                  pl.BlockSpec((ck, bn), lambda j, i: (i, nxt(j))),
                  pl.BlockSpec((ck, bn), lambda j, i: (i, nxt(j))),
                  pl.BlockSpec((FFN_CONV_WIDTH, bn), lambda j, i: (0, j)),
                  pl.BlockSpec((1, bn), lambda j, i: (0, j))],
        out_specs=pl.BlockSpec((bm, bn), lambda j, i: (i, j)),
        scratch_shapes=[pltpu.VMEM((SUBLANES, bn), F32), pltpu.VMEM((2, k, bn), BF16),
                        pltpu.VMEM((2, k, bn), BF16)],
        compiler_params=_cparams("arbitrary", "arbitrary"),
        name="ffn_up",
    )(h, wg[:, :bn].astype(BF16), wu[:, :bn].astype(BF16), wg, wu, conv_w, conv_b.reshape(1, n))


def _gelu_tanh(x):
    c = math.sqrt(2.0 / math.pi)
    hx = 0.5 * x
    return hx + hx * jnp.tanh(x * (c + (0.044715 * c) * (x * x)))


def _softplus(y):
    return jnp.maximum(y, 0.0) + jnp.log1p(jnp.exp(-jnp.abs(y)))


def _lru_kernel(x_ref, gt_ref, cw_ref, cb_ref, wa_ref, ba_ref, wi_ref, bi_ref, lam_ref,
                o_ref, prevx_ref, h_ref):
    t = pl.program_id(2)

    @pl.when(t == 0)
    def _():
        prevx_ref[...] = jnp.zeros_like(prevx_ref)
        h_ref[...] = jnp.zeros_like(h_ref)

    x = x_ref[...]
    rows, width = x.shape
    prev8 = prevx_ref[...]
    prevx_ref[...] = x[-SUBLANES:, :]
    cw = cw_ref[...]
    xc = cw[3:4, :] * x + cb_ref[...]
    for s in range(1, LRU_CONV_WIDTH):
        xc = xc + cw[LRU_CONV_WIDTH - 1 - s:LRU_CONV_WIDTH - s, :] * _shift_rows(x, prev8, s)

    xcb = xc.astype(BF16)
    heads = width // LRU_HEAD_DIM
    pre_a = []
    pre_i = []
    for hd in range(heads):
        xh = xcb[:, hd * LRU_HEAD_DIM:(hd + 1) * LRU_HEAD_DIM]
        pre_a.append(_dot(xh, wa_ref[hd]))
        pre_i.append(_dot(xh, wi_ref[hd]))
    gi = _sigmoid(jnp.concatenate(pre_i, axis=1) + bi_ref[...])
    half_rate = (-0.5 * LRU_C) * _softplus(-lam_ref[...])
    log_a = half_rate * jnp.tanh(0.5 * (jnp.concatenate(pre_a, axis=1) + ba_ref[...])) + half_rate
    a = jnp.exp(log_a)
    th = jnp.tanh(log_a)
    num = -2.0 * th
    mult = jnp.where(num > 0.0, num * lax.rsqrt(num * (1.0 - th)), 0.0)
    u = mult * (gi * xc)

    n_groups = rows // SUBLANES
    a = a.reshape(n_groups, SUBLANES, width)
    u = u.reshape(n_groups, SUBLANES, width)
    sub = lax.broadcasted_iota(jnp.int32, (n_groups, SUBLANES, width), 1)
    d = 1
    while d < SUBLANES:
        keep = sub >= d
        a_sh = jnp.where(keep, pltpu.roll(a, d, 1), 1.0)
        u_sh = jnp.where(keep, pltpu.roll(u, d, 1), 0.0)
        u = u + a * u_sh
        a = a * a_sh
        d *= 2
    carry = h_ref[...]
    groups = []
    for gi in range(n_groups):
        hg = u[gi] + a[gi] * carry
        groups.append(hg)
        carry = hg[SUBLANES - 1:SUBLANES, :]
    h = jnp.concatenate(groups, axis=0)
    h_ref[...] = carry
    o_ref[...] = (h * _gelu_tanh(gt_ref[...])).astype(o_ref.dtype)


def _lru(z, conv_w, conv_b, w_a, b_a, w_i, b_i, lam, batch, seq, lru_width, bt, bw):
    m = z.shape[0]
    nt = seq // bt
    ncb = lru_width // bw
    hpb = bw // LRU_HEAD_DIM
    row = lambda b, c, t: b * nt + t
    vec = pl.BlockSpec((1, bw), lambda b, c, t: (0, c))
    return pl.pallas_call(
        _lru_kernel,
        out_shape=jax.ShapeDtypeStruct((m, lru_width), BF16),
        grid=(batch, ncb, nt),
        in_specs=[pl.BlockSpec((bt, bw), lambda b, c, t: (row(b, c, t), c)),
                  pl.BlockSpec((bt, bw), lambda b, c, t: (row(b, c, t), ncb + c)),
                  pl.BlockSpec((LRU_CONV_WIDTH, bw), lambda b, c, t: (0, c)),
                  vec,
                  pl.BlockSpec((hpb, LRU_HEAD_DIM, LRU_HEAD_DIM), lambda b, c, t: (c, 0, 0)),
                  vec,
                  pl.BlockSpec((hpb, LRU_HEAD_DIM, LRU_HEAD_DIM), lambda b, c, t: (c, 0, 0)),
                  vec, vec],
        out_specs=pl.BlockSpec((bt, bw), lambda b, c, t: (row(b, c, t), c)),
        scratch_shapes=[pltpu.VMEM((SUBLANES, bw), F32), pltpu.VMEM((1, bw), F32)],
        compiler_params=_cparams("parallel", "parallel", "arbitrary"),
        name="rg_lru",
    )(z, z, conv_w, conv_b.reshape(1, -1), w_a.astype(BF16), b_a.reshape(1, -1),
      w_i.astype(BF16), b_i.reshape(1, -1), lam.reshape(1, -1))


def _split_bf16(x):
    hi = x.astype(BF16)
    lo = (x - hi.astype(F32)).astype(BF16)
    return hi, lo


def _rwkv_kernel(zr_ref, zk_ref, zv_ref, zt_ref, mur_ref, muk_ref, muv_ref, mut_ref,
                 w0_ref, w2_ref, a0_ref, a2_ref, g2_ref, kk_ref, ka_ref, rk_ref, gnw_ref, gnb_ref,
                 o_ref, pr_ref, pk_ref, pv_ref, pt_ref, state_ref, *, n_chunks):
    C = RWKV_CHUNK
    C2 = 2 * C
    gw = o_ref.shape[1]
    pairs = gw // LANES
    seg_w = 2 * LANES

    @pl.when(pl.program_id(2) == 0)
    def _():
        pr_ref[...] = jnp.zeros_like(pr_ref)
        pk_ref[...] = jnp.zeros_like(pk_ref)
        pv_ref[...] = jnp.zeros_like(pv_ref)
        pt_ref[...] = jnp.zeros_like(pt_ref)
        state_ref[...] = jnp.zeros_like(state_ref)

    ri = lax.broadcasted_iota(jnp.int32, (C2, C2), 0)
    ci = lax.broadcasted_iota(jnp.int32, (C2, C2), 1)
    strict = ((ri >= C) == (ci >= C)) & (ri > ci)
    rt_i = lax.broadcasted_iota(jnp.int32, (C, C2), 0)
    ct_i = lax.broadcasted_iota(jnp.int32, (C, C2), 1)
    incl = jnp.where(ct_i >= C, ct_i - C, ct_i) <= rt_i
    tri3 = (lax.broadcasted_iota(jnp.int32, (C, 3 * C), 0)
            >= lax.broadcasted_iota(jnp.int32, (C, 3 * C), 1) % C).astype(BF16)
    head0 = lax.broadcasted_iota(jnp.int32, (C, LANES), 1) < RWKV_HEAD_DIM
    seg = ((lax.broadcasted_iota(jnp.int32, (seg_w, seg_w), 0) // RWKV_HEAD_DIM)
           == (lax.broadcasted_iota(jnp.int32, (seg_w, seg_w), 1) // RWKV_HEAD_DIM)).astype(BF16)
    first_row = lax.broadcasted_iota(jnp.int32, (C, 1), 0) == 0
    P = range(pairs)

    def pair(x, p):
        return x[:, p * LANES:(p + 1) * LANES]

    def stack(x):
        return jnp.concatenate([jnp.where(head0, x, 0.0), jnp.where(head0, 0.0, x)], axis=0)

    def head_sum(x):
        xb = x.astype(BF16)
        return jnp.concatenate(
            [_dot(xb[:, q * seg_w:(q + 1) * seg_w], seg) for q in range(gw // seg_w)], axis=1)

    def mix(z_ref, p_ref, mu_ref, rows):
        z = z_ref[rows, :]
        rolled = pltpu.roll(z, 1, 0)
        sh = jnp.where(first_row, p_ref[...], rolled)
        p_ref[...] = rolled[0:1, :]
        return z + (sh - z) * mu_ref[...]

    def body(cidx, carry):
        rows = pl.ds(pl.multiple_of(cidx * C, C), C)
        r = mix(zr_ref, pr_ref, mur_ref, rows)
        k = mix(zk_ref, pk_ref, muk_ref, rows)
        v = mix(zv_ref, pv_ref, muv_ref, rows)
        tl = mix(zt_ref, pt_ref, mut_ref, rows)
        zw = jnp.tanh(tl[:, :RANK_PAD]).astype(BF16)
        za = tl[:, RANK_PAD:2 * RANK_PAD].astype(BF16)
        zg = _sigmoid(tl[:, 2 * RANK_PAD:]).astype(BF16)
        logw = (-math.exp(-0.5)) * _sigmoid(w0_ref[...] + _dot(zw, w2_ref[...]))
        a = _sigmoid(a0_ref[...] + _dot(za, a2_ref[...]))
        g = _dot(zg, g2_ref[...])

        kkp = k * kk_ref[...]
        kk = kkp * jnp.minimum(lax.rsqrt(head_sum(kkp * kkp)), 1.0 / L2_EPS)
        kp = k * (1.0 + (a - 1.0) * ka_ref[...])
        b = kk * a
        bonus = head_sum(r * kp * rk_ref[...]) * v

        l1 = logw.astype(BF16)
        rem = logw - l1.astype(F32)
        l2 = rem.astype(BF16)
        l3 = (rem - l2.astype(F32)).astype(BF16)
        lw = _dot(tri3, jnp.concatenate([l1, l2, l3], axis=0))
        lw_end = lw[C - 1:C, :]
        e_neg = jnp.exp(-lw)
        e_end = jnp.exp(lw_end)
        e_rem = e_end * e_neg
        at = -kk * jnp.exp(lw - logw)
        rt = (r * jnp.exp(lw)).astype(BF16)
        kh = kp * e_neg
        bh = b * e_neg
        kb = kp * e_rem
        bb = b * e_rem

        at_s = [stack(pair(at, p)).astype(BF16) for p in P]
        v_s = [stack(pair(v, p)).astype(BF16) for p in P]
        lhs = [jnp.concatenate([at_s[p], pair(rt, p)], axis=0) for p in P]
        rhs = [jnp.concatenate([stack(pair(kh, p)), stack(pair(bh, p))], axis=0).astype(BF16) for p in P]
        gram = [_dot_nt(lhs[p], rhs[p]) for p in P]
        st = [state_ref[p] for p in P]
        proj = [_dot_nt(lhs[p], st[p].astype(BF16)) for p in P]
        a_ak = [jnp.where(strict, gram[p][:C2, :C2], 0.0).astype(BF16) for p in P]
        q = [jnp.where(strict, gram[p][:C2, C2:], 0.0).astype(BF16) for p in P]
        a_r = [jnp.concatenate([jnp.where(incl, gram[p][C2:, :C2], 0.0),
                                jnp.where(incl, gram[p][C2:, C2:], 0.0)], axis=1).astype(BF16) for p in P]

        u = [proj[p][:C2] + _dot(a_ak[p], v_s[p]) for p in P]
        steps = int(math.log2(C))
        for j in range(steps):
            if j < steps - 1:
                x = [_dot(q[p], jnp.concatenate([q[p], u[p].astype(BF16)], axis=1)) for p in P]
                u = [u[p] + x[p][:, C2:] for p in P]
                q = [x[p][:, :C2].astype(BF16) for p in P]
            else:
                u = [u[p] + _dot(q[p], u[p].astype(BF16)) for p in P]

        vu = [jnp.concatenate([v_s[p], u[p].astype(BF16)], axis=0) for p in P]
        y = jnp.concatenate([proj[p][C2:] + _dot(a_r[p], vu[p]) for p in P], axis=1)
        kbb = [jnp.concatenate([stack(pair(kb, p)), stack(pair(bb, p))], axis=0).astype(BF16) for p in P]
        for p in P:
            state_ref[p] = st[p] * pair(e_end, p) + _dot_tn(vu[p], kbb[p])

        mean = head_sum(y) * (1.0 / RWKV_HEAD_DIM)
        dlt = y - mean
        var = head_sum(dlt * dlt) * (1.0 / RWKV_HEAD_DIM)
        yn = dlt * lax.rsqrt(var + GN_EPS) * gnw_ref[...] + gnb_ref[...]
        o_ref[rows, :] = ((yn + bonus) * g).astype(o_ref.dtype)
        return carry

    lax.fori_loop(0, n_chunks, body, 0)


def _rwkv(z, z_tail, mu_r, mu_k, mu_v, mu_t, w0, w2, a0, a2, g2, k_k, k_a, r_k, gn_w, gn_b,
          batch, seq, col0, width, bt, gw):
    m = z.shape[0]
    nt = seq // bt
    ng = width // gw
    row = lambda b, g, t: b * nt + t
    cb = col0 // gw
    wb = width // gw
    zspec = lambda off: pl.BlockSpec((bt, gw), lambda b, g, t: (row(b, g, t), cb + off * wb + g))
    vec = pl.BlockSpec((1, gw), lambda b, g, t: (0, g))
    mat = lambda r: pl.BlockSpec((r, gw), lambda b, g, t: (0, g))
    kern = functools.partial(_rwkv_kernel, n_chunks=bt // RWKV_CHUNK)
    return pl.pallas_call(
        kern,
        out_shape=jax.ShapeDtypeStruct((m, width), BF16),
        grid=(batch, ng, nt),
        in_specs=[zspec(0), zspec(1), zspec(2),
                  pl.BlockSpec((bt, TAIL_WIDTH), lambda b, g, t: (row(b, g, t), 0)),
                  vec, vec, vec,
                  pl.BlockSpec((1, TAIL_WIDTH), lambda b, g, t: (0, 0)),
                  vec, mat(RANK_PAD), vec, mat(RANK_PAD), mat(GATE_RANK),
                  vec, vec, vec, vec, vec],
        out_specs=pl.BlockSpec((bt, gw), lambda b, g, t: (row(b, g, t), g)),
        scratch_shapes=[pltpu.VMEM((1, gw), F32), pltpu.VMEM((1, gw), F32), pltpu.VMEM((1, gw), F32),
                        pltpu.VMEM((1, TAIL_WIDTH), F32),
                        pltpu.VMEM((gw // LANES, LANES, LANES), F32)],
        compiler_params=_cparams("parallel", "parallel", "arbitrary"),
        name="rwkv7",
    )(z, z, z, z_tail, mu_r, mu_k, mu_v, mu_t, w0, w2, a0, a2, g2, k_k, k_a, r_k, gn_w, gn_b)


def _pad_rows(w, rows):
    return jnp.pad(w, ((0, rows - w.shape[0]), (0, 0)))


def _pad_cols(w, cols):
    return jnp.pad(w, ((0, 0), (0, cols - w.shape[1])))


def _tiles(m, seq):
    return dict(
        norm_bm=min(256, m),
        mm_bm=min(1024, m), inproj_bn=1024, outproj_bn=1024, down_bm=min(512, m), down_bn=512,
        ffn_bm=min(2048, seq), ffn_bn=256, ffn_sub=2,
        lru_bt=min(256, seq), lru_bw=512,
        rwkv_bt=min(256, seq), rwkv_gw=2048,
    )


def kernel(x, g_mix, w_in, conv_lru_w, conv_lru_b, lru_w_a, lru_b_a, lru_w_i, lru_b_i, lru_lambda, rwkv_mu, rwkv_w0, rwkv_w2, rwkv_a0, rwkv_a2, rwkv_g2, rwkv_k_k, rwkv_k_a, rwkv_r_k, rwkv_gn_w, rwkv_gn_b, w_out, g_ffn, w_ffn_gate, ffn_conv_w, ffn_conv_b, w_ffn_up, w_ffn_down, g_final):
    batch, seq, d_model = x.shape
    depth = w_in.shape[0]
    lru_w = lru_lambda.shape[1]
    rw = rwkv_w0.shape[1]
    m = batch * seq
    t = _tiles(m, seq)
    xf = x.reshape(m, d_model)
    c_rkv = 2 * lru_w
    c_lr = c_rkv + 3 * rw

    def row(vv):
        return vv.reshape(1, -1)

    for l in range(depth):
        wi_t = jnp.swapaxes(w_in[l], 0, 1)
        wt = wi_t[c_lr:, :]
        w_tail_t = jnp.concatenate(
            [_pad_rows(wt[:DECAY_RANK], RANK_PAD),
             _pad_rows(wt[DECAY_RANK:DECAY_RANK + ICLR_RANK], RANK_PAD),
             wt[DECAY_RANK + ICLR_RANK:]], axis=0)
        mu = rwkv_mu[l]
        mu_t = jnp.concatenate(
            [_pad_cols(row(mu[c_lr - c_rkv:c_lr - c_rkv + DECAY_RANK]), RANK_PAD),
             _pad_cols(row(mu[c_lr - c_rkv + DECAY_RANK:c_lr - c_rkv + DECAY_RANK + ICLR_RANK]), RANK_PAD),
             row(mu[c_lr - c_rkv + DECAY_RANK + ICLR_RANK:])], axis=1)

        h = _rmsnorm(xf, g_mix[l], BF16, t["norm_bm"])
        z = _matmul_wt_ahead(h, wi_t, c_lr, F32, t["mm_bm"], t["inproj_bn"], "inproj")
        z_tail = _matmul_wcast_t(h, w_tail_t, TAIL_WIDTH, F32, t["mm_bm"], TAIL_WIDTH, "inproj_tail")
        y_lru = _lru(z, conv_lru_w[l], conv_lru_b[l], lru_w_a[l], lru_b_a[l], lru_w_i[l], lru_b_i[l],
                     lru_lambda[l], batch, seq, lru_w, t["lru_bt"], t["lru_bw"])
        y_rwkv = _rwkv(z, z_tail, row(mu[:rw]), row(mu[rw:2 * rw]), row(mu[2 * rw:3 * rw]), mu_t,
                       row(rwkv_w0[l]), _pad_rows(rwkv_w2[l], RANK_PAD).astype(BF16),
                       row(rwkv_a0[l]), _pad_rows(rwkv_a2[l], RANK_PAD).astype(BF16),
                       rwkv_g2[l].astype(BF16), row(rwkv_k_k[l]), row(rwkv_k_a[l]), row(rwkv_r_k[l]),
                       row(rwkv_gn_w[l]), row(rwkv_gn_b[l]),
                       batch, seq, c_rkv, rw, t["rwkv_bt"], t["rwkv_gw"])
        xf = _outproj(y_lru, y_rwkv, w_out[l].astype(BF16), xf, t["mm_bm"], t["outproj_bn"])

        h = _rmsnorm(xf, g_ffn[l], BF16, t["norm_bm"])
        act = _ffn_up(h, w_ffn_gate[l], w_ffn_up[l], ffn_conv_w[l],
                      ffn_conv_b[l], seq, t["ffn_bm"], t["ffn_bn"], t["ffn_sub"])
        xf = _matmul_residual(act, w_ffn_down[l].astype(BF16), xf, t["down_bm"], t["down_bn"], "ffn_down")

    return _rmsnorm(xf, g_final, F32, t["norm_bm"]).reshape(batch, seq, d_model)
```

```python
import functools
import math

import jax
import jax.numpy as jnp
from jax import lax
from jax.experimental import pallas as pl
from jax.experimental.pallas import tpu as pltpu

F32 = jnp.float32
BF16 = jnp.bfloat16

V7X_VMEM_LIMIT_BYTES = 56 * 1024 * 1024
LANES = 128
SUBLANES = 8

LRU_HEAD_DIM = 128
LRU_CONV_WIDTH = 4
LRU_C = 8.0
RWKV_HEAD_DIM = 64
DECAY_RANK = 96
ICLR_RANK = 96
GATE_RANK = 256
FFN_CONV_WIDTH = 3
NORM_EPS = 1e-6
GN_EPS = 64e-5
L2_EPS = 1e-12
RANK_PAD = LANES
TAIL_WIDTH = 2 * RANK_PAD + GATE_RANK
RWKV_CHUNK = 64


def _cparams(*sem):
    return pltpu.CompilerParams(dimension_semantics=sem, vmem_limit_bytes=V7X_VMEM_LIMIT_BYTES)


def _dot(a, b):
    return jnp.dot(a, b, preferred_element_type=F32)


def _dot_nt(a, b):
    return lax.dot_general(a, b, (((1,), (1,)), ((), ())), preferred_element_type=F32)


def _dot_tn(a, b):
    return lax.dot_general(a, b, (((0,), (0,)), ((), ())), preferred_element_type=F32)


def _sigmoid(x):
    return 0.5 * jnp.tanh(0.5 * x) + 0.5


def _shift_rows(x, prev8, s):
    xe = jnp.concatenate([prev8, x], axis=0)
    return pltpu.roll(xe, s, 0)[SUBLANES:, :]


def _rmsnorm_kernel(x_ref, g_ref, o_ref):
    x = x_ref[...]
    ms = jnp.mean(x * x, axis=-1, keepdims=True)
    o_ref[...] = (x * lax.rsqrt(ms + NORM_EPS) * g_ref[...]).astype(o_ref.dtype)


def _rmsnorm(x2d, g, out_dtype, bm):
    m, d = x2d.shape
    return pl.pallas_call(
        _rmsnorm_kernel,
        out_shape=jax.ShapeDtypeStruct((m, d), out_dtype),
        grid=(m // bm,),
        in_specs=[pl.BlockSpec((bm, d), lambda i: (i, 0)),
                  pl.BlockSpec((1, d), lambda i: (0, 0))],
        out_specs=pl.BlockSpec((bm, d), lambda i: (i, 0)),
        compiler_params=_cparams("parallel"),
        name="rmsnorm",
    )(x2d, g.reshape(1, d))


def _mm_wt_ahead_kernel(a_ref, wc_ref, o_ref, wb_ref):
    j = pl.program_id(0)
    i = pl.program_id(1)
    ck = wc_ref.shape[1]
    piece = pl.ds(pl.multiple_of(i * ck, ck), ck)

    @pl.when(j == 0)
    def _():
        wb_ref[0, piece, :] = wc_ref[...].T.astype(BF16)

    @pl.when(j > 0)
    def _():
        slot = (j - 1) % 2
        o_ref[...] = _dot(a_ref[...], wb_ref[slot]).astype(o_ref.dtype)
        wb_ref[1 - slot, piece, :] = wc_ref[...].T.astype(BF16)


def _matmul_wt_ahead(a, wt_f32, n, out_dtype, bm, bn, name):
    m, k = a.shape
    n_i = m // bm
    n_j = n // bn
    ck = k // n_i
    row = lambda j, i: jnp.where(j > 0, i, 0)
    return pl.pallas_call(
        _mm_wt_ahead_kernel,
        out_shape=jax.ShapeDtypeStruct((m, n), out_dtype),
        grid=(n_j + 1, n_i),
        in_specs=[pl.BlockSpec((bm, k), lambda j, i: (row(j, i), 0)),
                  pl.BlockSpec((bn, ck), lambda j, i: (jnp.minimum(j, n_j - 1), i))],
        out_specs=pl.BlockSpec((bm, bn), lambda j, i: (row(j, i), jnp.maximum(j - 1, 0))),
        scratch_shapes=[pltpu.VMEM((2, k, bn), BF16)],
        compiler_params=_cparams("arbitrary", "arbitrary"),
        name=name,
    )(a, wt_f32)


def _mm_wcast_t_kernel(a_ref, wt_ref, o_ref, wb_ref, *, tb):
    @pl.when(pl.program_id(1) == 0)
    def _():
        for c in range(wt_ref.shape[1] // tb):
            wb_ref[c * tb:(c + 1) * tb, :] = wt_ref[:, c * tb:(c + 1) * tb].T.astype(BF16)

    o_ref[...] = _dot(a_ref[...], wb_ref[...]).astype(o_ref.dtype)


def _matmul_wcast_t(a, wt_f32, n, out_dtype, bm, bn, name):
    m, k = a.shape
    kern = functools.partial(_mm_wcast_t_kernel, tb=bn)
    return pl.pallas_call(
        kern,
        out_shape=jax.ShapeDtypeStruct((m, n), out_dtype),
        grid=(n // bn, m // bm),
        in_specs=[pl.BlockSpec((bm, k), lambda j, i: (i, 0)),
                  pl.BlockSpec((bn, k), lambda j, i: (j, 0))],
        out_specs=pl.BlockSpec((bm, bn), lambda j, i: (i, j)),
        scratch_shapes=[pltpu.VMEM((k, bn), BF16)],
        compiler_params=_cparams("parallel", "arbitrary"),
        name=name,
    )(a, wt_f32)


W_PIECE_ROWS = 256


def _mm_res_ahead_kernel(a_ref, *refs, n_w, n_i, n_pieces):
    wc_refs = refs[:n_w]
    r_ref, o_ref, wb_ref = refs[n_w:]
    j = pl.program_id(0)
    i = pl.program_id(1)

    def prepare(slot):
        for q, wc_ref in enumerate(wc_refs):
            p = jnp.minimum(i + q * n_i, n_pieces - 1)
            rows = pl.ds(pl.multiple_of(p * W_PIECE_ROWS, W_PIECE_ROWS), W_PIECE_ROWS)
            wb_ref[slot, rows, :] = wc_ref[...].astype(BF16)

    @pl.when(j == 0)
    def _():
        prepare(0)

    @pl.when(j > 0)
    def _():
        slot = (j - 1) % 2
        o_ref[...] = r_ref[...] + _dot(a_ref[...], wb_ref[slot])
        prepare(1 - slot)


def _matmul_residual_ahead(a, w_f32, res, bm, bn, name):
    m, k = a.shape
    n = w_f32.shape[1]
    assert k % W_PIECE_ROWS == 0
    n_i = m // bm
    n_j = n // bn
    n_pieces = k // W_PIECE_ROWS
    n_w = -(-n_pieces // n_i)
    row = lambda j, i: jnp.where(j > 0, i, 0)
    col = lambda j: jnp.maximum(j - 1, 0)
    piece = lambda q: (lambda j, i: (jnp.minimum(i + q * n_i, n_pieces - 1), jnp.minimum(j, n_j - 1)))
    kern = functools.partial(_mm_res_ahead_kernel, n_w=n_w, n_i=n_i, n_pieces=n_pieces)
    return pl.pallas_call(
        kern,
        out_shape=jax.ShapeDtypeStruct((m, n), F32),
        grid=(n_j + 1, n_i),
        in_specs=([pl.BlockSpec((bm, k), lambda j, i: (row(j, i), 0))]
                  + [pl.BlockSpec((W_PIECE_ROWS, bn), piece(q)) for q in range(n_w)]
                  + [pl.BlockSpec((bm, bn), lambda j, i: (row(j, i), col(j)))]),
        out_specs=pl.BlockSpec((bm, bn), lambda j, i: (row(j, i), col(j))),
        scratch_shapes=[pltpu.VMEM((2, k, bn), BF16)],
        compiler_params=_cparams("arbitrary", "arbitrary"),
        name=name,
    )(a, *([w_f32] * n_w), res)


def _outproj_kernel(a1_ref, a2_ref, w1_ref, w2_ref, r_ref, o_ref):
    acc = _dot(a1_ref[...], w1_ref[...]) + _dot(a2_ref[...], w2_ref[...])
    o_ref[...] = r_ref[...] + acc


def _outproj(y1, y2, w, res, bm, bn):
    m, k1 = y1.shape
    k2 = y2.shape[1]
    assert k1 == k2
    n = w.shape[1]
    return pl.pallas_call(
        _outproj_kernel,
        out_shape=jax.ShapeDtypeStruct((m, n), F32),
        grid=(m // bm, n // bn),
        in_specs=[pl.BlockSpec((bm, k1), lambda i, j: (i, 0)),
                  pl.BlockSpec((bm, k2), lambda i, j: (i, 0)),
                  pl.BlockSpec((k1, bn), lambda i, j: (0, j)),
                  pl.BlockSpec((k2, bn), lambda i, j: (1, j)),
                  pl.BlockSpec((bm, bn), lambda i, j: (i, j))],
        out_specs=pl.BlockSpec((bm, bn), lambda i, j: (i, j)),
        compiler_params=_cparams("parallel", "parallel"),
        name="outproj",
    )(y1, y2, w, w, res)


def _ffn_up_kernel(h_ref, wg0_ref, wu0_ref, wgc_ref, wuc_ref, cw_ref, cb_ref, o_ref,
                   prev_ref, wgb_ref, wub_ref, *, tiles_per_seq, n_sub):
    j = pl.program_id(0)
    i = pl.program_id(1)
    slot = j % 2
    ck = wgc_ref.shape[0]

    @pl.when((j == 0) & (i == 0))
    def _():
        wgb_ref[0] = wg0_ref[...]
        wub_ref[0] = wu0_ref[...]

    @pl.when(i % tiles_per_seq == 0)
    def _():
        prev_ref[...] = jnp.zeros_like(prev_ref)

    cw = cw_ref[...]
    prev8 = prev_ref[...]
    sub = o_ref.shape[0] // n_sub
    for s in range(n_sub):
        rows = slice(s * sub, (s + 1) * sub)
        h = h_ref[rows, :]
        gate = _dot(h, wgb_ref[slot])
        up = _dot(h, wub_ref[slot])
        conv = (cw[2:3, :] * gate + cw[1:2, :] * _shift_rows(gate, prev8, 1)
                + cw[0:1, :] * _shift_rows(gate, prev8, 2) + cb_ref[...])
        o_ref[rows, :] = (conv * _sigmoid(conv) * up).astype(o_ref.dtype)
        prev8 = gate[-SUBLANES:, :]
    prev_ref[...] = prev8

    chunk = pl.ds(pl.multiple_of(i * ck, ck), ck)
    wgb_ref[1 - slot, chunk, :] = wgc_ref[...].astype(BF16)
    wub_ref[1 - slot, chunk, :] = wuc_ref[...].astype(BF16)


def _ffn_up(h, wg, wu, conv_w, conv_b, seq, bm, bn, n_sub):
    m, k = h.shape
    n = wg.shape[1]
    n_i = m // bm
    n_j = n // bn
    ck = k // n_i
    nxt = lambda j: jnp.minimum(j + 1, n_j - 1)
    kern = functools.partial(_ffn_up_kernel, tiles_per_seq=seq // bm, n_sub=n_sub)
    return pl.pallas_call(
        kern,
        out_shape=jax.ShapeDtypeStruct((m, n), BF16),
        grid=(n_j, n_i),
        in_specs=[pl.BlockSpec((bm, k), lambda j, i: (i, 0)),
                  pl.BlockSpec((k, bn), lambda j, i: (0, 0)),
                  pl.BlockSpec((k, bn), lambda j, i: (0, 0)),
                  pl.BlockSpec((ck, bn), lambda j, i: (i, nxt(j))),
                  pl.BlockSpec((ck, bn), lambda j, i: (i, nxt(j))),
                  pl.BlockSpec((FFN_CONV_WIDTH, bn), lambda j, i: (0, j)),
                  pl.BlockSpec((1, bn), lambda j, i: (0, j))],
        out_specs=pl.BlockSpec((bm, bn), lambda j, i: (i, j)),
        scratch_shapes=[pltpu.VMEM((SUBLANES, bn), F32), pltpu.VMEM((2, k, bn), BF16),
                        pltpu.VMEM((2, k, bn), BF16)],
        compiler_params=_cparams("arbitrary", "arbitrary"),
        name="ffn_up",
    )(h, wg[:, :bn].astype(BF16), wu[:, :bn].astype(BF16), wg, wu, conv_w, conv_b.reshape(1, n))


def _gelu_tanh(x):
    c = math.sqrt(2.0 / math.pi)
    hx = 0.5 * x
    return hx + hx * jnp.tanh(x * (c + (0.044715 * c) * (x * x)))


def _softplus(y):
    return jnp.maximum(y, 0.0) + jnp.log1p(jnp.exp(-jnp.abs(y)))


def _lru_kernel(x_ref, gt_ref, cw_ref, cb_ref, wa_ref, ba_ref, wi_ref, bi_ref, lam_ref,
                o_ref, prevx_ref, h_ref):
    t = pl.program_id(2)

    @pl.when(t == 0)
    def _():
        prevx_ref[...] = jnp.zeros_like(prevx_ref)
        h_ref[...] = jnp.zeros_like(h_ref)

    x = x_ref[...]
    rows, width = x.shape
    prev8 = prevx_ref[...]
    prevx_ref[...] = x[-SUBLANES:, :]
    cw = cw_ref[...]
    xc = cw[3:4, :] * x + cb_ref[...]
    for s in range(1, LRU_CONV_WIDTH):
        xc = xc + cw[LRU_CONV_WIDTH - 1 - s:LRU_CONV_WIDTH - s, :] * _shift_rows(x, prev8, s)

    xcb = xc.astype(BF16)
    heads = width // LRU_HEAD_DIM
    pre_a = []
    pre_i = []
    for hd in range(heads):
        xh = xcb[:, hd * LRU_HEAD_DIM:(hd + 1) * LRU_HEAD_DIM]
        pre_a.append(_dot(xh, wa_ref[hd]))
        pre_i.append(_dot(xh, wi_ref[hd]))
    gi = _sigmoid(jnp.concatenate(pre_i, axis=1) + bi_ref[...])
    half_rate = (-0.5 * LRU_C) * _softplus(-lam_ref[...])
    log_a = half_rate * jnp.tanh(0.5 * (jnp.concatenate(pre_a, axis=1) + ba_ref[...])) + half_rate
    a = jnp.exp(log_a)
    th = jnp.tanh(log_a)
    num = -2.0 * th
    mult = jnp.where(num > 0.0, num * lax.rsqrt(num * (1.0 - th)), 0.0)
    u = mult * (gi * xc)

    n_groups = rows // SUBLANES
    a = a.reshape(n_groups, SUBLANES, width)
    u = u.reshape(n_groups, SUBLANES, width)
    sub = lax.broadcasted_iota(jnp.int32, (n_groups, SUBLANES, width), 1)
    d = 1
    while d < SUBLANES:
        keep = sub >= d
        a_sh = jnp.where(keep, pltpu.roll(a, d, 1), 1.0)
        u_sh = jnp.where(keep, pltpu.roll(u, d, 1), 0.0)
        u = u + a * u_sh
        a = a * a_sh
        d *= 2
    carry = h_ref[...]
    groups = []
    for gi in range(n_groups):
        hg = u[gi] + a[gi] * carry
        groups.append(hg)
        carry = hg[SUBLANES - 1:SUBLANES, :]
    h = jnp.concatenate(groups, axis=0)
    h_ref[...] = carry
    o_ref[...] = (h * _gelu_tanh(gt_ref[...])).astype(o_ref.dtype)


def _lru(z, conv_w, conv_b, w_a, b_a, w_i, b_i, lam, batch, seq, lru_width, bt, bw):
    m = z.shape[0]
    nt = seq // bt
    ncb = lru_width // bw
    hpb = bw // LRU_HEAD_DIM
    row = lambda b, c, t: b * nt + t
    vec = pl.BlockSpec((1, bw), lambda b, c, t: (0, c))
    return pl.pallas_call(
        _lru_kernel,
        out_shape=jax.ShapeDtypeStruct((m, lru_width), BF16),
        grid=(batch, ncb, nt),
        in_specs=[pl.BlockSpec((bt, bw), lambda b, c, t: (row(b, c, t), c)),
                  pl.BlockSpec((bt, bw), lambda b, c, t: (row(b, c, t), ncb + c)),
                  pl.BlockSpec((LRU_CONV_WIDTH, bw), lambda b, c, t: (0, c)),
                  vec,
                  pl.BlockSpec((hpb, LRU_HEAD_DIM, LRU_HEAD_DIM), lambda b, c, t: (c, 0, 0)),
                  vec,
                  pl.BlockSpec((hpb, LRU_HEAD_DIM, LRU_HEAD_DIM), lambda b, c, t: (c, 0, 0)),
                  vec, vec],
        out_specs=pl.BlockSpec((bt, bw), lambda b, c, t: (row(b, c, t), c)),
        scratch_shapes=[pltpu.VMEM((SUBLANES, bw), F32), pltpu.VMEM((1, bw), F32)],
        compiler_params=_cparams("parallel", "parallel", "arbitrary"),
        name="rg_lru",
    )(z, z, conv_w, conv_b.reshape(1, -1), w_a.astype(BF16), b_a.reshape(1, -1),
      w_i.astype(BF16), b_i.reshape(1, -1), lam.reshape(1, -1))


def _rwkv_kernel(zr_ref, zk_ref, zv_ref, zt_ref, mur_ref, muk_ref, muv_ref, mut_ref,
                 w0_ref, w2_ref, a0_ref, a2_ref, g2_ref, kk_ref, ka_ref, rk_ref, gnw_ref, gnb_ref,
                 o_ref, pr_ref, pk_ref, pv_ref, pt_ref, state_ref, *, n_chunks):
    C = RWKV_CHUNK
    C2 = 2 * C
    gw = o_ref.shape[1]
    pairs = gw // LANES
    seg_w = 2 * LANES

    @pl.when(pl.program_id(2) == 0)
    def _():
        pr_ref[...] = jnp.zeros_like(pr_ref)
        pk_ref[...] = jnp.zeros_like(pk_ref)
        pv_ref[...] = jnp.zeros_like(pv_ref)
        pt_ref[...] = jnp.zeros_like(pt_ref)
        state_ref[...] = jnp.zeros_like(state_ref)

    ri = lax.broadcasted_iota(jnp.int32, (C2, C2), 0)
    ci = lax.broadcasted_iota(jnp.int32, (C2, C2), 1)
    strict = ((ri >= C) == (ci >= C)) & (ri > ci)
    rt_i = lax.broadcasted_iota(jnp.int32, (C, C2), 0)
    ct_i = lax.broadcasted_iota(jnp.int32, (C, C2), 1)
    incl = jnp.where(ct_i >= C, ct_i - C, ct_i) <= rt_i
    tri3 = (lax.broadcasted_iota(jnp.int32, (C, 3 * C), 0)
            >= lax.broadcasted_iota(jnp.int32, (C, 3 * C), 1) % C).astype(BF16)
    head0 = lax.broadcasted_iota(jnp.int32, (C, LANES), 1) < RWKV_HEAD_DIM
    seg = ((lax.broadcasted_iota(jnp.int32, (seg_w, seg_w), 0) // RWKV_HEAD_DIM)
           == (lax.broadcasted_iota(jnp.int32, (seg_w, seg_w), 1) // RWKV_HEAD_DIM)).astype(BF16)
    first_row = lax.broadcasted_iota(jnp.int32, (C, 1), 0) == 0
    P = range(pairs)

    def pair(x, p):
        return x[:, p * LANES:(p + 1) * LANES]

    def stack(x):
        return jnp.concatenate([jnp.where(head0, x, 0.0), jnp.where(head0, 0.0, x)], axis=0)

    def head_sum(x):
        xb = x.astype(BF16)
        return jnp.concatenate(
            [_dot(xb[:, q * seg_w:(q + 1) * seg_w], seg) for q in range(gw // seg_w)], axis=1)

    def mix(z_ref, p_ref, mu_ref, rows):
        z = z_ref[rows, :]
        rolled = pltpu.roll(z, 1, 0)
        sh = jnp.where(first_row, p_ref[...], rolled)
        p_ref[...] = rolled[0:1, :]
        return z + (sh - z) * mu_ref[...]

    def body(cidx, carry):
        rows = pl.ds(pl.multiple_of(cidx * C, C), C)
        r = mix(zr_ref, pr_ref, mur_ref, rows)
        k = mix(zk_ref, pk_ref, muk_ref, rows)
        v = mix(zv_ref, pv_ref, muv_ref, rows)
        tl = mix(zt_ref, pt_ref, mut_ref, rows)
        zw = jnp.tanh(tl[:, :RANK_PAD]).astype(BF16)
        za = tl[:, RANK_PAD:2 * RANK_PAD].astype(BF16)
        zg = _sigmoid(tl[:, 2 * RANK_PAD:]).astype(BF16)
        logw = (-math.exp(-0.5)) * _sigmoid(w0_ref[...] + _dot(zw, w2_ref[...]))
        a = _sigmoid(a0_ref[...] + _dot(za, a2_ref[...]))
        g = _dot(zg, g2_ref[...])

        kkp = k * kk_ref[...]
        kk = kkp * jnp.minimum(lax.rsqrt(head_sum(kkp * kkp)), 1.0 / L2_EPS)
        kp = k * (1.0 + (a - 1.0) * ka_ref[...])
        b = kk * a
        bonus = head_sum(r * kp * rk_ref[...]) * v

        l1 = logw.astype(BF16)
        rem = logw - l1.astype(F32)
        l2 = rem.astype(BF16)
        l3 = (rem - l2.astype(F32)).astype(BF16)
        lw = _dot(tri3, jnp.concatenate([l1, l2, l3], axis=0))
        lw_end = lw[C - 1:C, :]
        e_neg = jnp.exp(-lw)
        e_end = jnp.exp(lw_end)
        e_rem = e_end * e_neg
        at = -kk * jnp.exp(lw - logw)
        rt = (r * jnp.exp(lw)).astype(BF16)
        kh = kp * e_neg
        bh = b * e_neg
        kb = kp * e_rem
        bb = b * e_rem

        at_s = [stack(pair(at, p)).astype(BF16) for p in P]
        v_s = [stack(pair(v, p)).astype(BF16) for p in P]
        lhs = [jnp.concatenate([at_s[p], pair(rt, p)], axis=0) for p in P]
        rhs = [jnp.concatenate([stack(pair(kh, p)), stack(pair(bh, p))], axis=0).astype(BF16) for p in P]
        gram = [_dot_nt(lhs[p], rhs[p]) for p in P]
        st = [state_ref[p] for p in P]
        proj = [_dot_nt(lhs[p], st[p].astype(BF16)) for p in P]
        a_ak = [jnp.where(strict, gram[p][:C2, :C2], 0.0).astype(BF16) for p in P]
        q = [jnp.where(strict, gram[p][:C2, C2:], 0.0).astype(BF16) for p in P]
        a_r = [jnp.concatenate([jnp.where(incl, gram[p][C2:, :C2], 0.0),
                                jnp.where(incl, gram[p][C2:, C2:], 0.0)], axis=1).astype(BF16) for p in P]

        u = [proj[p][:C2] + _dot(a_ak[p], v_s[p]) for p in P]
        steps = int(math.log2(C))
        for j in range(steps):
            if j < steps - 1:
                x = [_dot(q[p], jnp.concatenate([q[p], u[p].astype(BF16)], axis=1)) for p in P]
                u = [u[p] + x[p][:, C2:] for p in P]
                q = [x[p][:, :C2].astype(BF16) for p in P]
            else:
                u = [u[p] + _dot(q[p], u[p].astype(BF16)) for p in P]

        vu = [jnp.concatenate([v_s[p], u[p].astype(BF16)], axis=0) for p in P]
        y = jnp.concatenate([proj[p][C2:] + _dot(a_r[p], vu[p]) for p in P], axis=1)
        kbb = [jnp.concatenate([stack(pair(kb, p)), stack(pair(bb, p))], axis=0).astype(BF16) for p in P]
        for p in P:
            state_ref[p] = st[p] * pair(e_end, p) + _dot_tn(vu[p], kbb[p])

        mean = head_sum(y) * (1.0 / RWKV_HEAD_DIM)
        dlt = y - mean
        var = head_sum(dlt * dlt) * (1.0 / RWKV_HEAD_DIM)
        yn = dlt * lax.rsqrt(var + GN_EPS) * gnw_ref[...] + gnb_ref[...]
        o_ref[rows, :] = ((yn + bonus) * g).astype(o_ref.dtype)
        return carry

    lax.fori_loop(0, n_chunks, body, 0)


def _rwkv(z, z_tail, mu_r, mu_k, mu_v, mu_t, w0, w2, a0, a2, g2, k_k, k_a, r_k, gn_w, gn_b,
          batch, seq, col0, width, bt, gw):
    m = z.shape[0]
    nt = seq // bt
    ng = width // gw
    row = lambda b, g, t: b * nt + t
    cb = col0 // gw
    wb = width // gw
    zspec = lambda off: pl.BlockSpec((bt, gw), lambda b, g, t: (row(b, g, t), cb + off * wb + g))
    vec = pl.BlockSpec((1, gw), lambda b, g, t: (0, g))
    mat = lambda r: pl.BlockSpec((r, gw), lambda b, g, t: (0, g))
    kern = functools.partial(_rwkv_kernel, n_chunks=bt // RWKV_CHUNK)
    return pl.pallas_call(
        kern,
        out_shape=jax.ShapeDtypeStruct((m, width), BF16),
        grid=(batch, ng, nt),
        in_specs=[zspec(0), zspec(1), zspec(2),
                  pl.BlockSpec((bt, TAIL_WIDTH), lambda b, g, t: (row(b, g, t), 0)),
                  vec, vec, vec,
                  pl.BlockSpec((1, TAIL_WIDTH), lambda b, g, t: (0, 0)),
                  vec, mat(RANK_PAD), vec, mat(RANK_PAD), mat(GATE_RANK),
                  vec, vec, vec, vec, vec],
        out_specs=pl.BlockSpec((bt, gw), lambda b, g, t: (row(b, g, t), g)),
        scratch_shapes=[pltpu.VMEM((1, gw), F32), pltpu.VMEM((1, gw), F32), pltpu.VMEM((1, gw), F32),
                        pltpu.VMEM((1, TAIL_WIDTH), F32),
                        pltpu.VMEM((gw // LANES, LANES, LANES), F32)],
        compiler_params=_cparams("parallel", "parallel", "arbitrary"),
        name="rwkv7",
    )(z, z, z, z_tail, mu_r, mu_k, mu_v, mu_t, w0, w2, a0, a2, g2, k_k, k_a, r_k, gn_w, gn_b)


def _pad_rows(w, rows):
    return jnp.pad(w, ((0, rows - w.shape[0]), (0, 0)))


def _pad_cols(w, cols):
    return jnp.pad(w, ((0, 0), (0, cols - w.shape[1])))


def _tiles(m, seq):
    return dict(
        norm_bm=min(256, m),
        mm_bm=min(1024, m), inproj_bn=1024, outproj_bn=1024, down_bm=min(512, m), down_bn=512,
        ffn_bm=min(2048, seq), ffn_bn=256, ffn_sub=2,
        lru_bt=min(256, seq), lru_bw=512,
        rwkv_bt=min(256, seq), rwkv_gw=2048,
    )


def kernel(x, g_mix, w_in, conv_lru_w, conv_lru_b, lru_w_a, lru_b_a, lru_w_i, lru_b_i, lru_lambda, rwkv_mu, rwkv_w0, rwkv_w2, rwkv_a0, rwkv_a2, rwkv_g2, rwkv_k_k, rwkv_k_a, rwkv_r_k, rwkv_gn_w, rwkv_gn_b, w_out, g_ffn, w_ffn_gate, ffn_conv_w, ffn_conv_b, w_ffn_up, w_ffn_down, g_final):
    batch, seq, d_model = x.shape
    depth = w_in.shape[0]
    lru_w = lru_lambda.shape[1]
    rw = rwkv_w0.shape[1]
    m = batch * seq
    t = _tiles(m, seq)
    xf = x.reshape(m, d_model)
    c_rkv = 2 * lru_w
    c_lr = c_rkv + 3 * rw

    def row(vv):
        return vv.reshape(1, -1)

    for l in range(depth):
        wi_t = jnp.swapaxes(w_in[l], 0, 1)
        wt = wi_t[c_lr:, :]
        w_tail_t = jnp.concatenate(
            [_pad_rows(wt[:DECAY_RANK], RANK_PAD),
             _pad_rows(wt[DECAY_RANK:DECAY_RANK + ICLR_RANK], RANK_PAD),
             wt[DECAY_RANK + ICLR_RANK:]], axis=0)
        mu = rwkv_mu[l]
        mu_t = jnp.concatenate(
            [_pad_cols(row(mu[c_lr - c_rkv:c_lr - c_rkv + DECAY_RANK]), RANK_PAD),
             _pad_cols(row(mu[c_lr - c_rkv + DECAY_RANK:c_lr - c_rkv + DECAY_RANK + ICLR_RANK]), RANK_PAD),
             row(mu[c_lr - c_rkv + DECAY_RANK + ICLR_RANK:])], axis=1)

        h = _rmsnorm(xf, g_mix[l], BF16, t["norm_bm"])
        z = _matmul_wt_ahead(h, wi_t, c_lr, F32, t["mm_bm"], t["inproj_bn"], "inproj")
        z_tail = _matmul_wcast_t(h, w_tail_t, TAIL_WIDTH, F32, t["mm_bm"], TAIL_WIDTH, "inproj_tail")
        y_lru = _lru(z, conv_lru_w[l], conv_lru_b[l], lru_w_a[l], lru_b_a[l], lru_w_i[l], lru_b_i[l],
                     lru_lambda[l], batch, seq, lru_w, t["lru_bt"], t["lru_bw"])
        y_rwkv = _rwkv(z, z_tail, row(mu[:rw]), row(mu[rw:2 * rw]), row(mu[2 * rw:3 * rw]), mu_t,
                       row(rwkv_w0[l]), _pad_rows(rwkv_w2[l], RANK_PAD).astype(BF16),
                       row(rwkv_a0[l]), _pad_rows(rwkv_a2[l], RANK_PAD).astype(BF16),
                       rwkv_g2[l].astype(BF16), row(rwkv_k_k[l]), row(rwkv_k_a[l]), row(rwkv_r_k[l]),
                       row(rwkv_gn_w[l]), row(rwkv_gn_b[l]),
                       batch, seq, c_rkv, rw, t["rwkv_bt"], t["rwkv_gw"])
        xf = _outproj(y_lru, y_rwkv, w_out[l].astype(BF16), xf, t["mm_bm"], t["outproj_bn"])

        h = _rmsnorm(xf, g_ffn[l], BF16, t["norm_bm"])
        act = _ffn_up(h, w_ffn_gate[l], w_ffn_up[l], ffn_conv_w[l],
                      ffn_conv_b[l], seq, t["ffn_bm"], t["ffn_bn"], t["ffn_sub"])
        xf = _matmul_residual_ahead(act, w_ffn_down[l], xf, t["down_bm"], t["down_bn"], "ffn_down")

    return _rmsnorm(xf, g_final, F32, t["norm_bm"]).reshape(batch, seq, d_model)
```

```python
import functools
import math

import jax
import jax.numpy as jnp
from jax import lax
from jax.experimental import pallas as pl
from jax.experimental.pallas import tpu as pltpu

F32 = jnp.float32
BF16 = jnp.bfloat16

V7X_VMEM_LIMIT_BYTES = 56 * 1024 * 1024
LANES = 128
SUBLANES = 8

LRU_HEAD_DIM = 128
LRU_CONV_WIDTH = 4
LRU_C = 8.0
RWKV_HEAD_DIM = 64
DECAY_RANK = 96
ICLR_RANK = 96
GATE_RANK = 256
FFN_CONV_WIDTH = 3
NORM_EPS = 1e-6
GN_EPS = 64e-5
L2_EPS = 1e-12
RANK_PAD = LANES
TAIL_WIDTH = 2 * RANK_PAD + GATE_RANK
RWKV_CHUNK = 64


def _cparams(*sem):
    return pltpu.CompilerParams(dimension_semantics=sem, vmem_limit_bytes=V7X_VMEM_LIMIT_BYTES)


def _dot(a, b):
    return jnp.dot(a, b, preferred_element_type=F32)


def _dot_nt(a, b):
    return lax.dot_general(a, b, (((1,), (1,)), ((), ())), preferred_element_type=F32)


def _dot_tn(a, b):
    return lax.dot_general(a, b, (((0,), (0,)), ((), ())), preferred_element_type=F32)


def _sigmoid(x):
    return 0.5 * jnp.tanh(0.5 * x) + 0.5


def _shift_rows(x, prev8, s):
    xe = jnp.concatenate([prev8, x], axis=0)
    return pltpu.roll(xe, s, 0)[SUBLANES:, :]


def _rmsnorm_kernel(x_ref, g_ref, o_ref):
    x = x_ref[...]
    ms = jnp.mean(x * x, axis=-1, keepdims=True)
    o_ref[...] = (x * lax.rsqrt(ms + NORM_EPS) * g_ref[...]).astype(o_ref.dtype)


def _rmsnorm(x2d, g, out_dtype, bm):
    m, d = x2d.shape
    return pl.pallas_call(
        _rmsnorm_kernel,
        out_shape=jax.ShapeDtypeStruct((m, d), out_dtype),
        grid=(m // bm,),
        in_specs=[pl.BlockSpec((bm, d), lambda i: (i, 0)),
                  pl.BlockSpec((1, d), lambda i: (0, 0))],
        out_specs=pl.BlockSpec((bm, d), lambda i: (i, 0)),
        compiler_params=_cparams("parallel"),
        name="rmsnorm",
    )(x2d, g.reshape(1, d))


def _mm_wt_ahead_kernel(a_ref, wc_ref, o_ref, wb_ref):
    j = pl.program_id(0)
    i = pl.program_id(1)
    ck = wc_ref.shape[1]
    piece = pl.ds(pl.multiple_of(i * ck, ck), ck)

    @pl.when(j == 0)
    def _():
        wb_ref[0, piece, :] = wc_ref[...].T.astype(BF16)

    @pl.when(j > 0)
    def _():
        slot = (j - 1) % 2
        o_ref[...] = _dot(a_ref[...], wb_ref[slot]).astype(o_ref.dtype)
        wb_ref[1 - slot, piece, :] = wc_ref[...].T.astype(BF16)


def _matmul_wt_ahead(a, wt_f32, n, out_dtype, bm, bn, name):
    m, k = a.shape
    n_i = m // bm
    n_j = n // bn
    ck = k // n_i
    row = lambda j, i: jnp.where(j > 0, i, 0)
    return pl.pallas_call(
        _mm_wt_ahead_kernel,
        out_shape=jax.ShapeDtypeStruct((m, n), out_dtype),
        grid=(n_j + 1, n_i),
        in_specs=[pl.BlockSpec((bm, k), lambda j, i: (row(j, i), 0)),
                  pl.BlockSpec((bn, ck), lambda j, i: (jnp.minimum(j, n_j - 1), i))],
        out_specs=pl.BlockSpec((bm, bn), lambda j, i: (row(j, i), jnp.maximum(j - 1, 0))),
        scratch_shapes=[pltpu.VMEM((2, k, bn), BF16)],
        compiler_params=_cparams("arbitrary", "arbitrary"),
        name=name,
    )(a, wt_f32)


def _mm_wcast_t_kernel(a_ref, wt_ref, o_ref, wb_ref, *, tb):
    @pl.when(pl.program_id(1) == 0)
    def _():
        for c in range(wt_ref.shape[1] // tb):
            wb_ref[c * tb:(c + 1) * tb, :] = wt_ref[:, c * tb:(c + 1) * tb].T.astype(BF16)

    o_ref[...] = _dot(a_ref[...], wb_ref[...]).astype(o_ref.dtype)


def _matmul_wcast_t(a, wt_f32, n, out_dtype, bm, bn, name):
    m, k = a.shape
    kern = functools.partial(_mm_wcast_t_kernel, tb=bn)
    return pl.pallas_call(
        kern,
        out_shape=jax.ShapeDtypeStruct((m, n), out_dtype),
        grid=(n // bn, m // bm),
        in_specs=[pl.BlockSpec((bm, k), lambda j, i: (i, 0)),
                  pl.BlockSpec((bn, k), lambda j, i: (j, 0))],
        out_specs=pl.BlockSpec((bm, bn), lambda j, i: (i, j)),
        scratch_shapes=[pltpu.VMEM((k, bn), BF16)],
        compiler_params=_cparams("parallel", "arbitrary"),
        name=name,
    )(a, wt_f32)


W_PIECE_ROWS = 256


def _mm_res_ahead_kernel(a_ref, *refs, n_w, n_i, n_pieces):
    wc_refs = refs[:n_w]
    r_ref, o_ref, wb_ref = refs[n_w:]
    j = pl.program_id(0)
    i = pl.program_id(1)

    def prepare(slot):
        for q, wc_ref in enumerate(wc_refs):
            p = jnp.minimum(i + q * n_i, n_pieces - 1)
            rows = pl.ds(pl.multiple_of(p * W_PIECE_ROWS, W_PIECE_ROWS), W_PIECE_ROWS)
            wb_ref[slot, rows, :] = wc_ref[...].astype(BF16)

    @pl.when(j == 0)
    def _():
        prepare(0)

    @pl.when(j > 0)
    def _():
        slot = (j - 1) % 2
        o_ref[...] = r_ref[...] + _dot(a_ref[...], wb_ref[slot])
        prepare(1 - slot)


def _matmul_residual_ahead(a, w_f32, res, bm, bn, name):
    m, k = a.shape
    n = w_f32.shape[1]
    assert k % W_PIECE_ROWS == 0
    n_i = m // bm
    n_j = n // bn
    n_pieces = k // W_PIECE_ROWS
    n_w = -(-n_pieces // n_i)
    row = lambda j, i: jnp.where(j > 0, i, 0)
    col = lambda j: jnp.maximum(j - 1, 0)
    piece = lambda q: (lambda j, i: (jnp.minimum(i + q * n_i, n_pieces - 1), jnp.minimum(j, n_j - 1)))
    kern = functools.partial(_mm_res_ahead_kernel, n_w=n_w, n_i=n_i, n_pieces=n_pieces)
    return pl.pallas_call(
        kern,
        out_shape=jax.ShapeDtypeStruct((m, n), F32),
        grid=(n_j + 1, n_i),
        in_specs=([pl.BlockSpec((bm, k), lambda j, i: (row(j, i), 0))]
                  + [pl.BlockSpec((W_PIECE_ROWS, bn), piece(q)) for q in range(n_w)]
                  + [pl.BlockSpec((bm, bn), lambda j, i: (row(j, i), col(j)))]),
        out_specs=pl.BlockSpec((bm, bn), lambda j, i: (row(j, i), col(j))),
        scratch_shapes=[pltpu.VMEM((2, k, bn), BF16)],
        compiler_params=_cparams("arbitrary", "arbitrary"),
        name=name,
    )(a, *([w_f32] * n_w), res)


def _outproj_kernel(a1_ref, a2_ref, w1_ref, w2_ref, r_ref, o_ref):
    acc = _dot(a1_ref[...], w1_ref[...]) + _dot(a2_ref[...], w2_ref[...])
    o_ref[...] = r_ref[...] + acc


def _outproj(y1, y2, w, res, bm, bn):
    m, k1 = y1.shape
    k2 = y2.shape[1]
    assert k1 == k2
    n = w.shape[1]
    return pl.pallas_call(
        _outproj_kernel,
        out_shape=jax.ShapeDtypeStruct((m, n), F32),
        grid=(m // bm, n // bn),
        in_specs=[pl.BlockSpec((bm, k1), lambda i, j: (i, 0)),
                  pl.BlockSpec((bm, k2), lambda i, j: (i, 0)),
                  pl.BlockSpec((k1, bn), lambda i, j: (0, j)),
                  pl.BlockSpec((k2, bn), lambda i, j: (1, j)),
                  pl.BlockSpec((bm, bn), lambda i, j: (i, j))],
        out_specs=pl.BlockSpec((bm, bn), lambda i, j: (i, j)),
        compiler_params=_cparams("parallel", "parallel"),
        name="outproj",
    )(y1, y2, w, w, res)


def _ffn_up_kernel(h_ref, wg0_ref, wu0_ref, wgc_ref, wuc_ref, cw_ref, cb_ref, o_ref,
                   prev_ref, wgb_ref, wub_ref, *, tiles_per_seq, n_sub):
    j = pl.program_id(0)
    i = pl.program_id(1)
    slot = j % 2
    ck = wgc_ref.shape[0]

    @pl.when((j == 0) & (i == 0))
    def _():
        wgb_ref[0] = wg0_ref[...]
        wub_ref[0] = wu0_ref[...]

    @pl.when(i % tiles_per_seq == 0)
    def _():
        prev_ref[...] = jnp.zeros_like(prev_ref)

    cw = cw_ref[...]
    prev8 = prev_ref[...]
    sub = o_ref.shape[0] // n_sub
    for s in range(n_sub):
        rows = slice(s * sub, (s + 1) * sub)
        h = h_ref[rows, :]
        gate = _dot(h, wgb_ref[slot])
        up = _dot(h, wub_ref[slot])
        conv = (cw[2:3, :] * gate + cw[1:2, :] * _shift_rows(gate, prev8, 1)
                + cw[0:1, :] * _shift_rows(gate, prev8, 2) + cb_ref[...])
        o_ref[rows, :] = (conv * _sigmoid(conv) * up).astype(o_ref.dtype)
        prev8 = gate[-SUBLANES:, :]
    prev_ref[...] = prev8

    chunk = pl.ds(pl.multiple_of(i * ck, ck), ck)
    wgb_ref[1 - slot, chunk, :] = wgc_ref[...].astype(BF16)
    wub_ref[1 - slot, chunk, :] = wuc_ref[...].astype(BF16)


def _ffn_up(h, wg, wu, conv_w, conv_b, seq, bm, bn, n_sub):
    m, k = h.shape
    n = wg.shape[1]
    n_i = m // bm
    n_j = n // bn
    ck = k // n_i
    nxt = lambda j: jnp.minimum(j + 1, n_j - 1)
    kern = functools.partial(_ffn_up_kernel, tiles_per_seq=seq // bm, n_sub=n_sub)
    return pl.pallas_call(
        kern,
        out_shape=jax.ShapeDtypeStruct((m, n), BF16),
        grid=(n_j, n_i),
        in_specs=[pl.BlockSpec((bm, k), lambda j, i: (i, 0)),
                  pl.BlockSpec((k, bn), lambda j, i: (0, 0)),
                  pl.BlockSpec((k, bn), lambda j, i: (0, 0)),
                  pl.BlockSpec((ck, bn), lambda j, i: (i, nxt(j))),
                  pl.BlockSpec((ck, bn), lambda j, i: (i, nxt(j))),
                  pl.BlockSpec((FFN_CONV_WIDTH, bn), lambda j, i: (0, j)),
                  pl.BlockSpec((1, bn), lambda j, i: (0, j))],
        out_specs=pl.BlockSpec((bm, bn), lambda j, i: (i, j)),
        scratch_shapes=[pltpu.VMEM((SUBLANES, bn), F32), pltpu.VMEM((2, k, bn), BF16),
                        pltpu.VMEM((2, k, bn), BF16)],
        compiler_params=_cparams("arbitrary", "arbitrary"),
        name="ffn_up",
    )(h, wg[:, :bn].astype(BF16), wu[:, :bn].astype(BF16), wg, wu, conv_w, conv_b.reshape(1, n))


def _gelu_tanh(x):
    c = math.sqrt(2.0 / math.pi)
    hx = 0.5 * x
    return hx + hx * jnp.tanh(x * (c + (0.044715 * c) * (x * x)))


def _softplus(y):
    return jnp.maximum(y, 0.0) + jnp.log1p(jnp.exp(-jnp.abs(y)))


def _lru_kernel(x_ref, gt_ref, cw_ref, cb_ref, wa_ref, ba_ref, wi_ref, bi_ref, lam_ref,
                o_ref, prevx_ref, h_ref):
    t = pl.program_id(2)

    @pl.when(t == 0)
    def _():
        prevx_ref[...] = jnp.zeros_like(prevx_ref)
        h_ref[...] = jnp.zeros_like(h_ref)

    x = x_ref[...]
    rows, width = x.shape
    prev8 = prevx_ref[...]
    prevx_ref[...] = x[-SUBLANES:, :]
    cw = cw_ref[...]
    xc = cw[3:4, :] * x + cb_ref[...]
    for s in range(1, LRU_CONV_WIDTH):
        xc = xc + cw[LRU_CONV_WIDTH - 1 - s:LRU_CONV_WIDTH - s, :] * _shift_rows(x, prev8, s)

    xcb = xc.astype(BF16)
    heads = width // LRU_HEAD_DIM
    pre_a = []
    pre_i = []
    for hd in range(heads):
        xh = xcb[:, hd * LRU_HEAD_DIM:(hd + 1) * LRU_HEAD_DIM]
        pre_a.append(_dot(xh, wa_ref[hd]))
        pre_i.append(_dot(xh, wi_ref[hd]))
    gi = _sigmoid(jnp.concatenate(pre_i, axis=1) + bi_ref[...])
    half_rate = (-0.5 * LRU_C) * _softplus(-lam_ref[...])
    log_a = half_rate * jnp.tanh(0.5 * (jnp.concatenate(pre_a, axis=1) + ba_ref[...])) + half_rate
    a = jnp.exp(log_a)
    th = jnp.tanh(log_a)
    num = -2.0 * th
    mult = jnp.where(num > 0.0, num * lax.rsqrt(num * (1.0 - th)), 0.0)
    u = mult * (gi * xc)

    n_groups = rows // SUBLANES
    a = a.reshape(n_groups, SUBLANES, width)
    u = u.reshape(n_groups, SUBLANES, width)
    sub = lax.broadcasted_iota(jnp.int32, (n_groups, SUBLANES, width), 1)
    d = 1
    while d < SUBLANES:
        keep = sub >= d
        a_sh = jnp.where(keep, pltpu.roll(a, d, 1), 1.0)
        u_sh = jnp.where(keep, pltpu.roll(u, d, 1), 0.0)
        u = u + a * u_sh
        a = a * a_sh
        d *= 2
    carry = h_ref[...]
    groups = []
    for gi in range(n_groups):
        hg = u[gi] + a[gi] * carry
        groups.append(hg)
        carry = hg[SUBLANES - 1:SUBLANES, :]
    h = jnp.concatenate(groups, axis=0)
    h_ref[...] = carry
    o_ref[...] = (h * _gelu_tanh(gt_ref[...])).astype(o_ref.dtype)


def _lru(z, conv_w, conv_b, w_a, b_a, w_i, b_i, lam, batch, seq, lru_width, bt, bw):
    m = z.shape[0]
    nt = seq // bt
    ncb = lru_width // bw
    hpb = bw // LRU_HEAD_DIM
    row = lambda b, c, t: b * nt + t
    vec = pl.BlockSpec((1, bw), lambda b, c, t: (0, c))
    return pl.pallas_call(
        _lru_kernel,
        out_shape=jax.ShapeDtypeStruct((m, lru_width), BF16),
        grid=(batch, ncb, nt),
        in_specs=[pl.BlockSpec((bt, bw), lambda b, c, t: (row(b, c, t), c)),
                  pl.BlockSpec((bt, bw), lambda b, c, t: (row(b, c, t), ncb + c)),
                  pl.BlockSpec((LRU_CONV_WIDTH, bw), lambda b, c, t: (0, c)),
                  vec,
                  pl.BlockSpec((hpb, LRU_HEAD_DIM, LRU_HEAD_DIM), lambda b, c, t: (c, 0, 0)),
                  vec,
                  pl.BlockSpec((hpb, LRU_HEAD_DIM, LRU_HEAD_DIM), lambda b, c, t: (c, 0, 0)),
                  vec, vec],
        out_specs=pl.BlockSpec((bt, bw), lambda b, c, t: (row(b, c, t), c)),
        scratch_shapes=[pltpu.VMEM((SUBLANES, bw), F32), pltpu.VMEM((1, bw), F32)],
        compiler_params=_cparams("parallel", "parallel", "arbitrary"),
        name="rg_lru",
    )(z, z, conv_w, conv_b.reshape(1, -1), w_a.astype(BF16), b_a.reshape(1, -1),
      w_i.astype(BF16), b_i.reshape(1, -1), lam.reshape(1, -1))


def _rwkv_kernel(zr_ref, zk_ref, zv_ref, zt_ref, mur_ref, muk_ref, muv_ref, mut_ref,
                 w0_ref, w2_ref, a0_ref, a2_ref, g2_ref, kk_ref, ka_ref, rk_ref, gnw_ref, gnb_ref,
                 o_ref, pr_ref, pk_ref, pv_ref, pt_ref, state_ref, *, n_chunks):
    C = RWKV_CHUNK
    C2 = 2 * C
    gw = o_ref.shape[1]
    pairs = gw // LANES
    seg_w = 2 * LANES

    @pl.when(pl.program_id(2) == 0)
    def _():
        pr_ref[...] = jnp.zeros_like(pr_ref)
        pk_ref[...] = jnp.zeros_like(pk_ref)
        pv_ref[...] = jnp.zeros_like(pv_ref)
        pt_ref[...] = jnp.zeros_like(pt_ref)
        state_ref[...] = jnp.zeros_like(state_ref)

    ri = lax.broadcasted_iota(jnp.int32, (C2, C2), 0)
    ci = lax.broadcasted_iota(jnp.int32, (C2, C2), 1)
    strict = ((ri >= C) == (ci >= C)) & (ri > ci)
    rt_i = lax.broadcasted_iota(jnp.int32, (C, C2), 0)
    ct_i = lax.broadcasted_iota(jnp.int32, (C, C2), 1)
    incl = jnp.where(ct_i >= C, ct_i - C, ct_i) <= rt_i
    tri3 = (lax.broadcasted_iota(jnp.int32, (C, 3 * C), 0)
            >= lax.broadcasted_iota(jnp.int32, (C, 3 * C), 1) % C).astype(BF16)
    head0 = lax.broadcasted_iota(jnp.int32, (C, LANES), 1) < RWKV_HEAD_DIM
    seg = ((lax.broadcasted_iota(jnp.int32, (seg_w, seg_w), 0) // RWKV_HEAD_DIM)
           == (lax.broadcasted_iota(jnp.int32, (seg_w, seg_w), 1) // RWKV_HEAD_DIM)).astype(BF16)
    first_row = lax.broadcasted_iota(jnp.int32, (C, 1), 0) == 0
    P = range(pairs)

    def pair(x, p):
        return x[:, p * LANES:(p + 1) * LANES]

    def stack(x):
        return jnp.concatenate([jnp.where(head0, x, 0.0), jnp.where(head0, 0.0, x)], axis=0)

    def head_sum(x):
        xb = x.astype(BF16)
        return jnp.concatenate(
            [_dot(xb[:, q * seg_w:(q + 1) * seg_w], seg) for q in range(gw // seg_w)], axis=1)

    def mix(z_ref, p_ref, mu_ref, rows):
        z = z_ref[rows, :]
        rolled = pltpu.roll(z, 1, 0)
        sh = jnp.where(first_row, p_ref[...], rolled)
        p_ref[...] = rolled[0:1, :]
        return z + (sh - z) * mu_ref[...]

    def body(cidx, carry):
        rows = pl.ds(pl.multiple_of(cidx * C, C), C)
        r = mix(zr_ref, pr_ref, mur_ref, rows)
        k = mix(zk_ref, pk_ref, muk_ref, rows)
        v = mix(zv_ref, pv_ref, muv_ref, rows)
        tl = mix(zt_ref, pt_ref, mut_ref, rows)
        zw = jnp.tanh(tl[:, :RANK_PAD]).astype(BF16)
        za = tl[:, RANK_PAD:2 * RANK_PAD].astype(BF16)
        zg = _sigmoid(tl[:, 2 * RANK_PAD:]).astype(BF16)
        logw = (-math.exp(-0.5)) * _sigmoid(w0_ref[...] + _dot(zw, w2_ref[...]))
        a = _sigmoid(a0_ref[...] + _dot(za, a2_ref[...]))
        g = _dot(zg, g2_ref[...])

        kkp = k * kk_ref[...]
        kk = kkp * jnp.minimum(lax.rsqrt(head_sum(kkp * kkp)), 1.0 / L2_EPS)
        kp = k * (1.0 + (a - 1.0) * ka_ref[...])
        b = kk * a
        bonus = head_sum(r * kp * rk_ref[...]) * v

        l1 = logw.astype(BF16)
        rem = logw - l1.astype(F32)
        l2 = rem.astype(BF16)
        l3 = (rem - l2.astype(F32)).astype(BF16)
        lw = _dot(tri3, jnp.concatenate([l1, l2, l3], axis=0))
        lw_end = lw[C - 1:C, :]
        e_neg = jnp.exp(-lw)
        e_end = jnp.exp(lw_end)
        e_rem = e_end * e_neg
        at = -kk * jnp.exp(lw - logw)
        rt = (r * jnp.exp(lw)).astype(BF16)
        kh = kp * e_neg
        bh = b * e_neg
        kb = kp * e_rem
        bb = b * e_rem

        at_s = [stack(pair(at, p)).astype(BF16) for p in P]
        v_s = [stack(pair(v, p)).astype(BF16) for p in P]
        lhs = [jnp.concatenate([at_s[p], pair(rt, p)], axis=0) for p in P]
        rhs = [jnp.concatenate([stack(pair(kh, p)), stack(pair(bh, p))], axis=0).astype(BF16) for p in P]
        gram = [_dot_nt(lhs[p], rhs[p]) for p in P]
        st = [state_ref[p] for p in P]
        proj = [_dot_nt(lhs[p], st[p].astype(BF16)) for p in P]
        a_ak = [jnp.where(strict, gram[p][:C2, :C2], 0.0).astype(BF16) for p in P]
        q = [jnp.where(strict, gram[p][:C2, C2:], 0.0).astype(BF16) for p in P]
        a_r = [jnp.concatenate([jnp.where(incl, gram[p][C2:, :C2], 0.0),
                                jnp.where(incl, gram[p][C2:, C2:], 0.0)], axis=1).astype(BF16) for p in P]

        u = [proj[p][:C2] + _dot(a_ak[p], v_s[p]) for p in P]
        steps = int(math.log2(C))
        for j in range(steps):
            if j < steps - 1:
                x = [_dot(q[p], jnp.concatenate([q[p], u[p].astype(BF16)], axis=1)) for p in P]
                u = [u[p] + x[p][:, C2:] for p in P]
                q = [x[p][:, :C2].astype(BF16) for p in P]
            else:
                u = [u[p] + _dot(q[p], u[p].astype(BF16)) for p in P]

        vu = [jnp.concatenate([v_s[p], u[p].astype(BF16)], axis=0) for p in P]
        y = jnp.concatenate([proj[p][C2:] + _dot(a_r[p], vu[p]) for p in P], axis=1)
        kbb = [jnp.concatenate([stack(pair(kb, p)), stack(pair(bb, p))], axis=0).astype(BF16) for p in P]
        for p in P:
            state_ref[p] = st[p] * pair(e_end, p) + _dot_tn(vu[p], kbb[p])

        mean = head_sum(y) * (1.0 / RWKV_HEAD_DIM)
        dlt = y - mean
        var = head_sum(dlt * dlt) * (1.0 / RWKV_HEAD_DIM)
        yn = dlt * lax.rsqrt(var + GN_EPS) * gnw_ref[...] + gnb_ref[...]
        o_ref[rows, :] = ((yn + bonus) * g).astype(o_ref.dtype)
        return carry

    lax.fori_loop(0, n_chunks, body, 0)


def _rwkv(z, z_tail, mu_r, mu_k, mu_v, mu_t, w0, w2, a0, a2, g2, k_k, k_a, r_k, gn_w, gn_b,
          batch, seq, col0, width, bt, gw):
    m = z.shape[0]
    nt = seq // bt
    ng = width // gw
    row = lambda b, g, t: b * nt + t
    cb = col0 // gw
    wb = width // gw
    zspec = lambda off: pl.BlockSpec((bt, gw), lambda b, g, t: (row(b, g, t), cb + off * wb + g))
    vec = pl.BlockSpec((1, gw), lambda b, g, t: (0, g))
    mat = lambda r: pl.BlockSpec((r, gw), lambda b, g, t: (0, g))
    kern = functools.partial(_rwkv_kernel, n_chunks=bt // RWKV_CHUNK)
    return pl.pallas_call(
        kern,
        out_shape=jax.ShapeDtypeStruct((m, width), BF16),
        grid=(batch, ng, nt),
        in_specs=[zspec(0), zspec(1), zspec(2),
                  pl.BlockSpec((bt, TAIL_WIDTH), lambda b, g, t: (row(b, g, t), 0)),
                  vec, vec, vec,
                  pl.BlockSpec((1, TAIL_WIDTH), lambda b, g, t: (0, 0)),
                  vec, mat(RANK_PAD), vec, mat(RANK_PAD), mat(GATE_RANK),
                  vec, vec, vec, vec, vec],
        out_specs=pl.BlockSpec((bt, gw), lambda b, g, t: (row(b, g, t), g)),
        scratch_shapes=[pltpu.VMEM((1, gw), F32), pltpu.VMEM((1, gw), F32), pltpu.VMEM((1, gw), F32),
                        pltpu.VMEM((1, TAIL_WIDTH), F32),
                        pltpu.VMEM((gw // LANES, LANES, LANES), F32)],
        compiler_params=_cparams("parallel", "parallel", "arbitrary"),
        name="rwkv7",
    )(z, z, z, z_tail, mu_r, mu_k, mu_v, mu_t, w0, w2, a0, a2, g2, k_k, k_a, r_k, gn_w, gn_b)


def _pad_rows(w, rows):
    return jnp.pad(w, ((0, rows - w.shape[0]), (0, 0)))


def _pad_cols(w, cols):
    return jnp.pad(w, ((0, 0), (0, cols - w.shape[1])))


def _tiles(m, seq):
    return dict(
        norm_bm=min(512, m),
        mm_bm=min(1024, m), inproj_bn=1024, outproj_bn=1024, down_bm=min(512, m), down_bn=512,
        ffn_bm=min(2048, seq), ffn_bn=256, ffn_sub=2,
        lru_bt=min(256, seq), lru_bw=512,
        rwkv_bt=min(256, seq), rwkv_gw=2048,
    )


def kernel(x, g_mix, w_in, conv_lru_w, conv_lru_b, lru_w_a, lru_b_a, lru_w_i, lru_b_i, lru_lambda, rwkv_mu, rwkv_w0, rwkv_w2, rwkv_a0, rwkv_a2, rwkv_g2, rwkv_k_k, rwkv_k_a, rwkv_r_k, rwkv_gn_w, rwkv_gn_b, w_out, g_ffn, w_ffn_gate, ffn_conv_w, ffn_conv_b, w_ffn_up, w_ffn_down, g_final):
    batch, seq, d_model = x.shape
    depth = w_in.shape[0]
    lru_w = lru_lambda.shape[1]
    rw = rwkv_w0.shape[1]
    m = batch * seq
    t = _tiles(m, seq)
    xf = x.reshape(m, d_model)
    c_rkv = 2 * lru_w
    c_lr = c_rkv + 3 * rw

    def row(vv):
        return vv.reshape(1, -1)

    for l in range(depth):
        wi_t = jnp.swapaxes(w_in[l], 0, 1)
        wt = wi_t[c_lr:, :]
        w_tail_t = jnp.concatenate(
            [_pad_rows(wt[:DECAY_RANK], RANK_PAD),
             _pad_rows(wt[DECAY_RANK:DECAY_RANK + ICLR_RANK], RANK_PAD),
             wt[DECAY_RANK + ICLR_RANK:]], axis=0)
        mu = rwkv_mu[l]
        mu_t = jnp.concatenate(
            [_pad_cols(row(mu[c_lr - c_rkv:c_lr - c_rkv + DECAY_RANK]), RANK_PAD),
             _pad_cols(row(mu[c_lr - c_rkv + DECAY_RANK:c_lr - c_rkv + DECAY_RANK + ICLR_RANK]), RANK_PAD),
             row(mu[c_lr - c_rkv + DECAY_RANK + ICLR_RANK:])], axis=1)

        h = _rmsnorm(xf, g_mix[l], BF16, t["norm_bm"])
        z = _matmul_wt_ahead(h, wi_t, c_lr, F32, t["mm_bm"], t["inproj_bn"], "inproj")
        z_tail = _matmul_wcast_t(h, w_tail_t, TAIL_WIDTH, F32, t["mm_bm"], TAIL_WIDTH, "inproj_tail")
        y_lru = _lru(z, conv_lru_w[l], conv_lru_b[l], lru_w_a[l], lru_b_a[l], lru_w_i[l], lru_b_i[l],
                     lru_lambda[l], batch, seq, lru_w, t["lru_bt"], t["lru_bw"])
        y_rwkv = _rwkv(z, z_tail, row(mu[:rw]), row(mu[rw:2 * rw]), row(mu[2 * rw:3 * rw]), mu_t,
                       row(rwkv_w0[l]), _pad_rows(rwkv_w2[l], RANK_PAD).astype(BF16),
                       row(rwkv_a0[l]), _pad_rows(rwkv_a2[l], RANK_PAD).astype(BF16),
                       rwkv_g2[l].astype(BF16), row(rwkv_k_k[l]), row(rwkv_k_a[l]), row(rwkv_r_k[l]),
                       row(rwkv_gn_w[l]), row(rwkv_gn_b[l]),
                       batch, seq, c_rkv, rw, t["rwkv_bt"], t["rwkv_gw"])
        xf = _outproj(y_lru, y_rwkv, w_out[l].astype(BF16), xf, t["mm_bm"], t["outproj_bn"])

        h = _rmsnorm(xf, g_ffn[l], BF16, t["norm_bm"])
        act = _ffn_up(h, w_ffn_gate[l], w_ffn_up[l], ffn_conv_w[l],
                      ffn_conv_b[l], seq, t["ffn_bm"], t["ffn_bn"], t["ffn_sub"])
        xf = _matmul_residual_ahead(act, w_ffn_down[l], xf, t["down_bm"], t["down_bn"], "ffn_down")

    return _rmsnorm(xf, g_final, F32, t["norm_bm"]).reshape(batch, seq, d_model)
```
